```python
import math
import jax
import jax.numpy as jnp
from jax import lax
import numpy as np

D_MODEL = 1024
BATCH = 4
SEQ = 4096
DEPTH = 4
DEC_BATCH = 128
DEC_SEQ = 4
PAST_LEN = 8192
PAGE_SIZE = 128

N_ATT = (DEPTH + 1) // 2
N_SSM = DEPTH // 2
MLA_HEADS = 8
Q_RANK = 256
KV_RANK = 128
NOPE_DIM = 64
ROPE_DIM = 32
V_DIM = 64
MLA_SCALE = (NOPE_DIM + ROPE_DIM) ** -0.5
MOBA_HEADS = 8
MOBA_DIM = 64
MOBA_WIDTH = MOBA_HEADS * MOBA_DIM
MOBA_BLOCK = 256
MOBA_TOPK = 3
MOBA_SCALE = MOBA_DIM ** -0.5
ROPE_THETA = 10000.0
Q_BLOCK = 128
IN_COLS = Q_RANK + KV_RANK + ROPE_DIM + 3 * MOBA_WIDTH
ATT_OUT = MLA_HEADS * V_DIM + MOBA_WIDTH
SSM_GROUP = 16
SSM_GROUPS = D_MODEL // SSM_GROUP
SSM_STATE = 64
D_FF = 2816
CONV_W = 3
ALPHA = (2 * DEPTH) ** 0.25
BETA = (8 * DEPTH) ** -0.25
LN_EPS = 1e-5
RMS_EPS = 1e-6

kernel_name = "hybrid_mla_moba_s5_convffn_step"


def layer_norm(x, g, b):
    xf = x.astype(jnp.float32)
    mu = xf.mean(-1, keepdims=True)
    var = jnp.square(xf - mu).mean(-1, keepdims=True)
    return ((xf - mu) * lax.rsqrt(var + LN_EPS) * g + b).astype(x.dtype)


def rms_norm(x, g):
    xf = x.astype(jnp.float32)
    return (xf * lax.rsqrt(jnp.square(xf).mean(-1, keepdims=True) + RMS_EPS) * g).astype(x.dtype)


def rope(x, pos):
    d = x.shape[-1]
    half = d // 2
    inv = ROPE_THETA ** (-jnp.arange(half, dtype=jnp.float32) * 2.0 / d)
    ang = pos.astype(jnp.float32)[:, None] * inv
    shape = (ang.shape[0],) + (1,) * (x.ndim - 3) + (half,)
    cos = jnp.cos(ang).reshape(shape)
    sin = jnp.sin(ang).reshape(shape)
    xf = x.astype(jnp.float32)
    x1, x2 = xf[..., :half], xf[..., half:]
    return jnp.concatenate([x1 * cos - x2 * sin, x1 * sin + x2 * cos], -1).astype(x.dtype)


def attn_project(x, pos, w_in, q_norm, w_q_b, kv_norm, w_kv_b):
    B, T, _ = x.shape
    h = x @ w_in
    o1 = Q_RANK
    o2 = o1 + KV_RANK
    o3 = o2 + ROPE_DIM
    o4 = o3 + MOBA_WIDTH
    o5 = o4 + MOBA_WIDTH
    q_a, c_kv, k_pe = h[..., :o1], h[..., o1:o2], h[..., o2:o3]
    qm, km, vm = h[..., o3:o4], h[..., o4:o5], h[..., o5:]
    q = (rms_norm(q_a, q_norm) @ w_q_b).reshape(B, T, MLA_HEADS, NOPE_DIM + ROPE_DIM)
    q_lat = jnp.einsum('bthn,chn->bthc', q[..., :NOPE_DIM], w_kv_b[..., :NOPE_DIM])
    q_mla = jnp.concatenate([q_lat, rope(q[..., NOPE_DIM:], pos)], -1)
    c_kv = rms_norm(c_kv, kv_norm)
    k_pe = rope(k_pe, pos)
    qm = rope(qm.reshape(B, T, MOBA_HEADS, MOBA_DIM), pos)
    km = rope(km.reshape(B, T, MOBA_HEADS, MOBA_DIM), pos)
    vm = vm.reshape(B, T, MOBA_HEADS, MOBA_DIM)
    return q_mla, c_kv, k_pe, qm, km, vm


def attn_merge(o_mla, o_moba, w_kv_b, w_out):
    B, T = o_mla.shape[:2]
    v_mla = jnp.einsum('bthc,chv->bthv', o_mla, w_kv_b[..., NOPE_DIM:]).reshape(B, T, MLA_HEADS * V_DIM)
    return jnp.concatenate([v_mla, o_moba.reshape(B, T, MOBA_WIDTH)], -1) @ w_out


def dense_causal_attend(q, k, v, qpos):
    s = jnp.einsum('thc,lc->htl', q, k, preferred_element_type=jnp.float32) * MLA_SCALE
    mask = jnp.arange(k.shape[0])[None, :] <= qpos[:, None]
    s = jnp.where(mask[None], s, -jnp.inf)
    p = jax.nn.softmax(s, axis=-1)
    return jnp.einsum('htl,ld->thd', p.astype(v.dtype), v)


def mla_prompt(q, kcat, lat):
    B, S = q.shape[:2]
    nb = S // Q_BLOCK
    qb = q.reshape(B, nb, Q_BLOCK, MLA_HEADS, q.shape[-1]).swapaxes(0, 1)
    starts = jnp.arange(nb, dtype=jnp.int32) * Q_BLOCK

    def one(args):
        st, qi = args
        qpos = st + jnp.arange(Q_BLOCK, dtype=jnp.int32)
        return jax.vmap(dense_causal_attend, in_axes=(0, 0, 0, None))(qi, kcat, lat, qpos)

    o = lax.map(one, (starts, qb))
    return o.swapaxes(0, 1).reshape(B, S, MLA_HEADS, KV_RANK)


def moba_blocks(k, v):
    L = k.shape[0]
    nblk = -(-L // MOBA_BLOCK)
    pad = nblk * MOBA_BLOCK - L
    kb = jnp.pad(k, ((0, pad), (0, 0), (0, 0))).reshape(nblk, MOBA_BLOCK, MOBA_HEADS, MOBA_DIM).transpose(2, 0, 1, 3)
    vb = jnp.pad(v, ((0, pad), (0, 0), (0, 0))).reshape(nblk, MOBA_BLOCK, MOBA_HEADS, MOBA_DIM).transpose(2, 0, 1, 3)
    kmean = kb.astype(jnp.float32).mean(axis=2)
    return kb, vb, kmean


def moba_attend(q, qpos, kb, vb, kmean):
    H, nblk = kb.shape[:2]
    T = q.shape[0]
    qh = q.swapaxes(0, 1)
    own = qpos // MOBA_BLOCK
    gate = jnp.einsum('htd,hnd->htn', qh.astype(jnp.float32), kmean)
    past = jnp.arange(nblk)[None, :] < own[:, None]
    gate = jnp.where(past[None], gate, -jnp.inf)
    n_sel = min(MOBA_TOPK, nblk)
    top_s, top_i = lax.top_k(gate, n_sel)
    sel = jnp.concatenate([top_i.astype(jnp.int32), jnp.broadcast_to(own.astype(jnp.int32)[None, :, None], (H, T, 1))], -1)
    ok = jnp.concatenate([jnp.isfinite(top_s), jnp.ones((H, T, 1), dtype=bool)], -1)
    kg = jax.vmap(lambda kh, ih: kh[ih])(kb, sel)
    vg = jax.vmap(lambda vh, ih: vh[ih])(vb, sel)
    s = jnp.einsum('htd,htjkd->htjk', qh, kg, preferred_element_type=jnp.float32) * MOBA_SCALE
    kpos = sel[..., None] * MOBA_BLOCK + jnp.arange(MOBA_BLOCK, dtype=jnp.int32)
    mask = ok[..., None] & (kpos <= qpos[None, :, None, None])
    s = jnp.where(mask, s, -jnp.inf)
    p = jax.nn.softmax(s.reshape(H, T, -1), axis=-1).reshape(s.shape)
    o = jnp.einsum('htjk,htjkd->thd', p.astype(vg.dtype), vg)
    return o.reshape(T, H * MOBA_DIM)


def moba_prompt(q, k, v):
    B, S = q.shape[:2]
    nb = S // Q_BLOCK
    kb, vb, km = jax.vmap(moba_blocks)(k, v)
    qs = q.reshape(B * nb, Q_BLOCK, MOBA_HEADS, MOBA_DIM)
    bidx = jnp.repeat(jnp.arange(B, dtype=jnp.int32), nb)
    starts = jnp.tile(jnp.arange(nb, dtype=jnp.int32) * Q_BLOCK, B)

    def one(args):
        b, st, qi = args
        return moba_attend(qi, st + jnp.arange(Q_BLOCK, dtype=jnp.int32), kb[b], vb[b], km[b])

    o = lax.map(one, (bidx, starts, qs))
    return o.reshape(B, S, MOBA_WIDTH)


def attn_sample(page_table, q_mla, c_kv, k_pe, qm, km, vm, pool_lat, pool_rope, pool_k, pool_v, qpos):
    def one(args):
        pt, qa, ca, ra, qb, kb_, vb_ = args
        lat = jnp.concatenate([pool_lat[pt].reshape(-1, KV_RANK).astype(ca.dtype), ca], 0)
        rp = jnp.concatenate([pool_rope[pt].reshape(-1, ROPE_DIM).astype(ra.dtype), ra], 0)
        o_mla = dense_causal_attend(qa, jnp.concatenate([lat, rp], -1), lat, qpos)
        kk = jnp.concatenate([pool_k[pt].reshape(-1, MOBA_HEADS, MOBA_DIM).astype(kb_.dtype), kb_], 0)
        vv = jnp.concatenate([pool_v[pt].reshape(-1, MOBA_HEADS, MOBA_DIM).astype(vb_.dtype), vb_], 0)
        kbk, vbk, kmean = moba_blocks(kk, vv)
        o_moba = moba_attend(qb, qpos, kbk, vbk, kmean)
        return o_mla, o_moba

    return lax.map(one, (page_table, q_mla, c_kv, k_pe, qm, km, vm))


def _ssm_combine(left, right):
    a1, b1 = left
    a2, b2 = right
    return a2 * a1, a2 * b1 + b2


def s5_mix(x, h0_re, h0_im, a_re, a_im, log_dt, b_re, b_im, c_re, c_im, d_skip, w_glu):
    B, T, _ = x.shape
    f32 = jnp.float32
    lam = lax.complex(a_re.astype(f32), a_im.astype(f32))
    dt = jnp.exp(log_dt.astype(f32))[:, None]
    abar = jnp.exp(lam * dt)
    bbar = ((abar - 1.0) / lam)[..., None] * lax.complex(b_re.astype(f32), b_im.astype(f32))
    u = x.astype(f32).reshape(B, T, SSM_GROUPS, SSM_GROUP)
    bu = jnp.einsum('gpc,btgc->btgp', bbar, u)
    h0 = lax.complex(h0_re.astype(f32), h0_im.astype(f32))
    bu = bu.at[:, 0].add(abar * h0)
    a_el = jnp.broadcast_to(abar, bu.shape)
    _, h = lax.associative_scan(_ssm_combine, (a_el, bu), axis=1)
    cmat = lax.complex(c_re.astype(f32), c_im.astype(f32))
    y = jnp.einsum('gcp,btgp->btgc', cmat, h).real.reshape(B, T, D_MODEL) + d_skip.astype(f32) * x.astype(f32)
    z = jax.nn.gelu(y).astype(x.dtype)
    zz = z @ w_glu
    out = zz[..., :D_MODEL] * jax.nn.sigmoid(zz[..., D_MODEL:])
    h_last = h[:, -1]
    return out, h_last.real, h_last.imag


def conv_ffn(x, buf, w_up, conv_w, conv_b, w_down):
    T = x.shape[1]
    u = x @ w_up
    ext = jnp.concatenate([buf.astype(u.dtype), u], axis=1)
    c = conv_b + conv_w[0] * ext[:, 0:T]
    for j in range(1, CONV_W):
        c = c + conv_w[j] * ext[:, j:j + T]
    out = (jax.nn.gelu(c[..., :D_FF]) * c[..., D_FF:]) @ w_down
    return out, ext[:, -(CONV_W - 1):]


def setup_inputs(seed: int = 0) -> dict:
    key = jax.random.key(seed)
    ks = jax.random.split(key, 40)
    f32 = jnp.float32

    def nrm(k, shape, s=1.0):
        return s * jax.random.normal(k, shape, f32)

    n_pages = PAST_LEN // PAGE_SIZE
    n_pool = (DEC_BATCH * n_pages * 5) // 4
    page_table = jax.random.permutation(ks[0], n_pool)[: DEC_BATCH * n_pages].reshape(DEC_BATCH, n_pages).astype(jnp.int32)
    glu_scale = jnp.concatenate([jnp.full((D_MODEL,), BETA, f32), jnp.ones((D_MODEL,), f32)])
    return {
        'x_prompt': nrm(ks[1], (BATCH, SEQ, D_MODEL)),
        'x_sample': nrm(ks[2], (DEC_BATCH, DEC_SEQ, D_MODEL)),
        'cache_mla_latent': nrm(ks[3], (N_ATT, n_pool, PAGE_SIZE, KV_RANK)),
        'cache_mla_rope': nrm(ks[4], (N_ATT, n_pool, PAGE_SIZE, ROPE_DIM)),
        'cache_moba_k': nrm(ks[5], (N_ATT, n_pool, PAGE_SIZE, MOBA_HEADS, MOBA_DIM)),
        'cache_moba_v': nrm(ks[6], (N_ATT, n_pool, PAGE_SIZE, MOBA_HEADS, MOBA_DIM)),
        'state_s5_re': nrm(ks[7], (N_SSM, DEC_BATCH, SSM_GROUPS, SSM_STATE), 0.1),
        'state_s5_im': nrm(ks[8], (N_SSM, DEC_BATCH, SSM_GROUPS, SSM_STATE), 0.1),
        'state_ffn_conv': nrm(ks[9], (DEPTH, DEC_BATCH, CONV_W - 1, 2 * D_FF)),
        'page_table': page_table,
        'ln_mix_g': 1.0 + nrm(ks[10], (DEPTH, D_MODEL), 0.01),
        'ln_mix_b': nrm(ks[11], (DEPTH, D_MODEL), 0.01),
        'ln_ffn_g': 1.0 + nrm(ks[12], (DEPTH, D_MODEL), 0.01),
        'ln_ffn_b': nrm(ks[13], (DEPTH, D_MODEL), 0.01),
        'att_w_in': nrm(ks[14], (N_ATT, D_MODEL, IN_COLS), D_MODEL ** -0.5),
        'mla_q_norm': 1.0 + nrm(ks[15], (N_ATT, Q_RANK), 0.01),
        'mla_w_q_b': nrm(ks[16], (N_ATT, Q_RANK, MLA_HEADS * (NOPE_DIM + ROPE_DIM)), Q_RANK ** -0.5),
        'mla_kv_norm': 1.0 + nrm(ks[17], (N_ATT, KV_RANK), 0.01),
        'mla_w_kv_b': nrm(ks[18], (N_ATT, KV_RANK, MLA_HEADS, NOPE_DIM + V_DIM), KV_RANK ** -0.5),
        'att_w_out': nrm(ks[19], (N_ATT, ATT_OUT, D_MODEL), BETA * ATT_OUT ** -0.5),
        's5_a_re': -0.5 + nrm(ks[20], (N_SSM, SSM_GROUPS, SSM_STATE), 0.01),
        's5_a_im': jnp.pi * jnp.arange(SSM_STATE, dtype=f32) + nrm(ks[21], (N_SSM, SSM_GROUPS, SSM_STATE), 0.01),
        's5_log_dt': jax.random.uniform(ks[22], (N_SSM, SSM_GROUPS), f32, math.log(1e-3), math.log(1e-1)),
        's5_b_re': nrm(ks[23], (N_SSM, SSM_GROUPS, SSM_STATE, SSM_GROUP), (2 * SSM_GROUP) ** -0.5),
        's5_b_im': nrm(ks[24], (N_SSM, SSM_GROUPS, SSM_STATE, SSM_GROUP), (2 * SSM_GROUP) ** -0.5),
        's5_c_re': nrm(ks[25], (N_SSM, SSM_GROUPS, SSM_GROUP, SSM_STATE), (2 * SSM_STATE) ** -0.5),
        's5_c_im': nrm(ks[26], (N_SSM, SSM_GROUPS, SSM_GROUP, SSM_STATE), (2 * SSM_STATE) ** -0.5),
        's5_d': nrm(ks[27], (N_SSM, D_MODEL), 0.5),
        's5_w_glu': nrm(ks[28], (N_SSM, D_MODEL, 2 * D_MODEL), D_MODEL ** -0.5) * glu_scale,
        'ffn_w_up': nrm(ks[29], (DEPTH, D_MODEL, 2 * D_FF), D_MODEL ** -0.5),
        'ffn_conv_w': nrm(ks[30], (DEPTH, CONV_W, 2 * D_FF), 0.5),
        'ffn_conv_b': nrm(ks[31], (DEPTH, 2 * D_FF), 0.02),
        'ffn_w_down': nrm(ks[32], (DEPTH, D_FF, D_MODEL), BETA * D_FF ** -0.5),
    }


def reference(x_prompt, x_sample, cache_mla_latent, cache_mla_rope, cache_moba_k, cache_moba_v,
              state_s5_re, state_s5_im, state_ffn_conv, page_table,
              ln_mix_g, ln_mix_b, ln_ffn_g, ln_ffn_b,
              att_w_in, mla_q_norm, mla_w_q_b, mla_kv_norm, mla_w_kv_b, att_w_out,
              s5_a_re, s5_a_im, s5_log_dt, s5_b_re, s5_b_im, s5_c_re, s5_c_im, s5_d, s5_w_glu,
              ffn_w_up, ffn_conv_w, ffn_conv_b, ffn_w_down):
    pos_p = jnp.arange(SEQ, dtype=jnp.int32)
    pos_s = PAST_LEN + jnp.arange(DEC_SEQ, dtype=jnp.int32)
    xp, xs = x_prompt, x_sample
    lat_p, rope_p, mk_p, mv_p, sre_p, sim_p, conv_p = [], [], [], [], [], [], []
    lat_s, rope_s, mk_s, mv_s, sre_s, sim_s, conv_s = [], [], [], [], [], [], []
    for l in range(DEPTH):
        if l % 2 == 0:
            a = l // 2
            qa, ca, ra, qb, kb, vb = attn_project(xp, pos_p, att_w_in[a], mla_q_norm[a], mla_w_q_b[a], mla_kv_norm[a], mla_w_kv_b[a])
            o_mla = mla_prompt(qa, jnp.concatenate([ca, ra], -1), ca)
            o_moba = moba_prompt(qb, kb, vb)
            mix_p = attn_merge(o_mla, o_moba, mla_w_kv_b[a], att_w_out[a])
            lat_p.append(ca); rope_p.append(ra); mk_p.append(kb); mv_p.append(vb)
            qa, ca, ra, qb, kb, vb = attn_project(xs, pos_s, att_w_in[a], mla_q_norm[a], mla_w_q_b[a], mla_kv_norm[a], mla_w_kv_b[a])
            o_mla, o_moba = attn_sample(page_table, qa, ca, ra, qb, kb, vb,
                                        cache_mla_latent[a], cache_mla_rope[a], cache_moba_k[a], cache_moba_v[a], pos_s)
            mix_s = attn_merge(o_mla, o_moba, mla_w_kv_b[a], att_w_out[a])
            lat_s.append(ca); rope_s.append(ra); mk_s.append(kb); mv_s.append(vb)
        else:
            s = l // 2
            h0 = jnp.zeros((xp.shape[0], SSM_GROUPS, SSM_STATE), jnp.float32)
            mix_p, hre, him = s5_mix(xp, h0, h0, s5_a_re[s], s5_a_im[s], s5_log_dt[s], s5_b_re[s], s5_b_im[s],
                                     s5_c_re[s], s5_c_im[s], s5_d[s], s5_w_glu[s])
            sre_p.append(hre); sim_p.append(him)
            mix_s, hre, him = s5_mix(xs, state_s5_re[s], state_s5_im[s], s5_a_re[s], s5_a_im[s], s5_log_dt[s],
                                     s5_b_re[s], s5_b_im[s], s5_c_re[s], s5_c_im[s], s5_d[s], s5_w_glu[s])
            sre_s.append(hre); sim_s.append(him)
        xp = layer_norm(ALPHA * xp + mix_p, ln_mix_g[l], ln_mix_b[l])
        xs = layer_norm(ALPHA * xs + mix_s, ln_mix_g[l], ln_mix_b[l])
        buf0 = jnp.zeros((xp.shape[0], CONV_W - 1, 2 * D_FF), xp.dtype)
        f_p, bp = conv_ffn(xp, buf0, ffn_w_up[l], ffn_conv_w[l], ffn_conv_b[l], ffn_w_down[l])
        f_s, bs = conv_ffn(xs, state_ffn_conv[l], ffn_w_up[l], ffn_conv_w[l], ffn_conv_b[l], ffn_w_down[l])
        conv_p.append(bp); conv_s.append(bs)
        xp = layer_norm(ALPHA * xp + f_p, ln_ffn_g[l], ln_ffn_b[l])
        xs = layer_norm(ALPHA * xs + f_s, ln_ffn_g[l], ln_ffn_b[l])
    return (xp, xs,
            jnp.stack(lat_p), jnp.stack(rope_p), jnp.stack(mk_p), jnp.stack(mv_p),
            jnp.stack(sre_p), jnp.stack(sim_p), jnp.stack(conv_p),
            jnp.stack(lat_s), jnp.stack(rope_s), jnp.stack(mk_s), jnp.stack(mv_s),
            jnp.stack(sre_s), jnp.stack(sim_s), jnp.stack(conv_s))
```

```python
import functools
import math

import jax
import jax.numpy as jnp
from jax import lax
from jax.experimental import pallas as pl
from jax.experimental.pallas import tpu as pltpu

F32 = jnp.float32
BF16 = jnp.bfloat16

D_MODEL = 1024
PAGE_SIZE = 128
MLA_HEADS = 8
Q_RANK = 256
KV_RANK = 128
NOPE_DIM = 64
ROPE_DIM = 32
V_DIM = 64
MLA_SCALE = (NOPE_DIM + ROPE_DIM) ** -0.5
MOBA_HEADS = 8
MOBA_DIM = 64
MOBA_WIDTH = MOBA_HEADS * MOBA_DIM
MOBA_BLOCK = 256
MOBA_TOPK = 3
MOBA_SCALE = MOBA_DIM ** -0.5
ROPE_THETA = 10000.0
SSM_GROUP = 16
SSM_GROUPS = D_MODEL // SSM_GROUP
SSM_STATE = 64
SSM_CH = SSM_GROUPS * SSM_STATE
D_FF = 2816
CONV_W = 3
DEPTH = 4
ALPHA = (2 * DEPTH) ** 0.25
LN_EPS = 1e-5
RMS_EPS = 1e-6

LANE = 128
SUBLANE = 8
NEG = -1e30
LOG2E = math.log2(math.e)
ROW_TILE = 256
MLA_TQ = 128
MLA_TK = 512
S5_CHUNKS = 4
S5_CIN = D_MODEL // S5_CHUNKS
S5_CST = SSM_CH // S5_CHUNKS
FFN_CW = 256
PAGES_PER_STEP = 8
W_IN_COLS = Q_RANK + KV_RANK + 3 * MOBA_WIDTH + LANE
VMEM_LIMIT = 56 * 1024 * 1024


def _cparams(sem):
    return pltpu.CompilerParams(dimension_semantics=sem, vmem_limit_bytes=VMEM_LIMIT)


def _bdot(a, b):
    return jnp.dot(a.astype(BF16), b.astype(BF16), preferred_element_type=F32)


def _bdot_nt(a, b):
    return lax.dot_general(a.astype(BF16), b.astype(BF16), (((1,), (1,)), ((), ())),
                           preferred_element_type=F32)


def _layer_norm(v, g, b):
    mu = jnp.mean(v, axis=-1, keepdims=True)
    d = v - mu
    var = jnp.mean(d * d, axis=-1, keepdims=True)
    return d * lax.rsqrt(var + LN_EPS) * g + b


def _rms_norm(v, g):
    return v * lax.rsqrt(jnp.mean(v * v, axis=-1, keepdims=True) + RMS_EPS) * g


def _gelu(v):
    return 0.5 * v * (1.0 + jnp.tanh(math.sqrt(2.0 / math.pi) * (v + 0.044715 * (v * v * v))))


def _sigmoid(v):
    return 1.0 / (1.0 + jnp.exp(-v))


def _swap_halves(v, half):
    w = v.shape[-1]
    lane = lax.broadcasted_iota(jnp.int32, v.shape, 1)
    fwd = pltpu.roll(v, w - half, axis=1)
    bwd = pltpu.roll(v, half, axis=1)
    return jnp.where((lane % (2 * half)) < half, fwd, bwd)


def _rope(v, cos, sin_signed, half):
    outs = []
    for c in range(v.shape[-1] // LANE):
        vc = v[:, c * LANE:(c + 1) * LANE]
        outs.append(vc * cos + _swap_halves(vc, half) * sin_signed)
    return outs[0] if len(outs) == 1 else jnp.concatenate(outs, axis=-1)


def _rope_table_kernel(inv64_ref, sg64_ref, inv32_ref, sg32_ref, c64_ref, s64_ref, c32_ref, s32_ref, *, offset):
    rows = c64_ref.shape[0]
    base = pl.program_id(0) * rows + offset
    pos = (base + lax.broadcasted_iota(jnp.int32, (rows, LANE), 0)).astype(F32)
    a64 = pos * inv64_ref[...]
    c64_ref[...] = jnp.cos(a64)
    s64_ref[...] = jnp.sin(a64) * sg64_ref[...]
    a32 = pos * inv32_ref[...]
    c32_ref[...] = jnp.cos(a32)
    s32_ref[...] = jnp.sin(a32) * sg32_ref[...]


def _rope_tables(n_rows, offset):
    def lane_consts(d):
        half = d // 2
        inv = ROPE_THETA ** (-jnp.arange(half, dtype=F32) * 2.0 / d)
        l = jnp.arange(LANE)
        return (inv[(l % d) % half].reshape(1, LANE),
                jnp.where((l % d) < half, -1.0, 1.0).astype(F32).reshape(1, LANE))
    inv64, sg64 = lane_consts(MOBA_DIM)
    inv32, sg32 = lane_consts(ROPE_DIM)
    rows = min(n_rows, 512)
    assert n_rows % rows == 0 and rows % SUBLANE == 0
    const = pl.BlockSpec((1, LANE), lambda i: (0, 0))
    out = pl.BlockSpec((rows, LANE), lambda i: (i, 0))
    return pl.pallas_call(
        functools.partial(_rope_table_kernel, offset=offset),
        out_shape=[jax.ShapeDtypeStruct((n_rows, LANE), F32)] * 4,
        grid=(n_rows // rows,),
        in_specs=[const] * 4,
        out_specs=[out] * 4,
        compiler_params=_cparams(("arbitrary",)),
        name="rope_tables",
    )(inv64, sg64, inv32, sg32)


def _proj_kernel(x_ref, win_ref, qn_ref, wqb_ref, wk_ref, kvn_ref, c64_ref, s64_ref, c32_ref, s32_ref,
                 qcat_ref, kcat_ref, lat_ref, rope_ref, qm_ref, km_ref, kaug_ref, vm_ref, vmb_ref, kmean_ref,
                 *, blocks_per_seq):
    tm = x_ref.shape[0]
    nsub = tm // LANE
    c64, s64, c32, s32 = c64_ref[...], s64_ref[...], c32_ref[...], s32_ref[...]
    h = _bdot(x_ref[...], win_ref[...])
    o_kv = Q_RANK
    o_qm = o_kv + KV_RANK
    o_km = o_qm + MOBA_WIDTH
    o_vm = o_km + MOBA_WIDTH
    o_pe = o_vm + MOBA_WIDTH

    qn = _rms_norm(h[:, :Q_RANK], qn_ref[...])
    q = _bdot(qn, wqb_ref[...])
    o_qr = MLA_HEADS * LANE
    qr = _rope(q[:, o_qr:o_qr + MLA_HEADS * ROPE_DIM], c32, s32, ROPE_DIM // 2)
    lane = lax.broadcasted_iota(jnp.int32, (tm, LANE), 1)
    heads_per_chunk = LANE // ROPE_DIM
    for hd in range(MLA_HEADS):
        ql = _bdot(q[:, hd * LANE:(hd + 1) * LANE], wk_ref[hd]).astype(BF16)
        grp = hd % heads_per_chunk
        chunk = qr[:, (hd // heads_per_chunk) * LANE:(hd // heads_per_chunk + 1) * LANE]
        qrh = jnp.where((lane >= grp * ROPE_DIM) & (lane < (grp + 1) * ROPE_DIM), chunk, 0.0).astype(BF16)
        for sb in range(nsub):
            qcat_ref[sb, hd, :, 0:LANE] = ql[sb * LANE:(sb + 1) * LANE]
            qcat_ref[sb, hd, :, LANE:2 * LANE] = qrh[sb * LANE:(sb + 1) * LANE]

    lat = _rms_norm(h[:, o_kv:o_kv + KV_RANK], kvn_ref[...])
    lat_ref[...] = lat
    kpe = _rope(h[:, o_pe:o_pe + LANE], c32, s32, ROPE_DIM // 2)
    rope_ref[...] = kpe[:, :ROPE_DIM]
    kcat_ref[:, 0:KV_RANK] = lat.astype(BF16)
    kcat_ref[:, KV_RANK:KV_RANK + LANE] = kpe.astype(BF16)

    qm_ref[...] = _rope(h[:, o_qm:o_qm + MOBA_WIDTH], c64, s64, MOBA_DIM // 2)
    km = _rope(h[:, o_km:o_km + MOBA_WIDTH], c64, s64, MOBA_DIM // 2)
    km_ref[...] = km
    kmean_ref[0] = jnp.mean(km, axis=0, keepdims=True)
    vm = h[:, o_vm:o_vm + MOBA_WIDTH]
    vm_ref[...] = vm
    vmb_ref[...] = vm.astype(BF16)
    blk = pl.program_id(0) % blocks_per_seq
    onehot = jnp.where(lane == MOBA_DIM + blk, 1.0, 0.0)
    for g in range(MOBA_HEADS // 2):
        ch = km[:, g * LANE:(g + 1) * LANE]
        kaug_ref[:, (2 * g) * LANE:(2 * g + 1) * LANE] = jnp.where(lane < MOBA_DIM, ch, onehot).astype(BF16)
        chs = pltpu.roll(ch, MOBA_DIM, axis=1)
        kaug_ref[:, (2 * g + 1) * LANE:(2 * g + 2) * LANE] = jnp.where(lane < MOBA_DIM, chs, onehot).astype(BF16)


def _attn_proj(x, w, tabs, seq_rows):
    n = x.shape[0]
    tm = ROW_TILE
    assert tm == MOBA_BLOCK and n % tm == 0 and seq_rows % tm == 0
    tiles_per_seq = seq_rows // tm
    nsub = tm // LANE
    c64, s64, c32, s32 = tabs
    row = lambda width: pl.BlockSpec((tm, width), lambda i: (i, 0))
    tab = pl.BlockSpec((tm, LANE), lambda i: (i % tiles_per_seq, 0))
    full = lambda a: pl.BlockSpec(a.shape, lambda i: (0,) * a.ndim)
    outs = pl.pallas_call(
        functools.partial(_proj_kernel, blocks_per_seq=tiles_per_seq),
        out_shape=[
            jax.ShapeDtypeStruct((n // LANE, MLA_HEADS, LANE, 2 * LANE), BF16),
            jax.ShapeDtypeStruct((n, 2 * LANE), BF16),
            jax.ShapeDtypeStruct((n, KV_RANK), F32),
            jax.ShapeDtypeStruct((n, ROPE_DIM), F32),
            jax.ShapeDtypeStruct((n, MOBA_WIDTH), F32),
            jax.ShapeDtypeStruct((n, MOBA_WIDTH), F32),
            jax.ShapeDtypeStruct((n, MOBA_HEADS * LANE), BF16),
            jax.ShapeDtypeStruct((n, MOBA_WIDTH), F32),
            jax.ShapeDtypeStruct((n, MOBA_WIDTH), BF16),
            jax.ShapeDtypeStruct((n // tm, 1, MOBA_WIDTH), F32),
        ],
        grid=(n // tm,),
        in_specs=[row(D_MODEL), full(w["w_in"]), full(w["q_norm"]), full(w["w_q_b"]), full(w["wk"]),
                  full(w["kv_norm"]), tab, tab, tab, tab],
        out_specs=[
            pl.BlockSpec((nsub, MLA_HEADS, LANE, 2 * LANE), lambda i: (i, 0, 0, 0)),
            row(2 * LANE), row(KV_RANK), row(ROPE_DIM), row(MOBA_WIDTH), row(MOBA_WIDTH),
            row(MOBA_HEADS * LANE), row(MOBA_WIDTH), row(MOBA_WIDTH),
            pl.BlockSpec((1, 1, MOBA_WIDTH), lambda i: (i, 0, 0)),
        ],
        compiler_params=_cparams(("arbitrary",)),
        name="attn_proj",
    )(x, w["w_in"], w["q_norm"], w["w_q_b"], w["wk"], w["kv_norm"], c64, s64, c32, s32)
    return outs


def _mla_prompt_kernel(q_ref, kcat_ref, o_ref, m_ref, l_ref, acc_ref):
    qi = pl.program_id(1)
    rows = MLA_HEADS * MLA_TQ
    q = q_ref[0].reshape(rows, 2 * LANE)
    cexp = MLA_SCALE * LOG2E

    def scores(kt):
        k = kcat_ref[pl.ds(pl.multiple_of(kt * MLA_TK, MLA_TK), MLA_TK), :]
        return _bdot_nt(q, k), k[:, :KV_RANK]

    kd = (qi * MLA_TQ) // MLA_TK
    s, v = scores(kd)
    col = kd * MLA_TK + lax.broadcasted_iota(jnp.int32, (rows, MLA_TK), 1)
    rowpos = qi * MLA_TQ + lax.broadcasted_iota(jnp.int32, (rows, MLA_TK), 0) % MLA_TQ
    s = jnp.where(col <= rowpos, s, NEG)
    m = jnp.max(s, axis=1, keepdims=True)
    p = jnp.exp2((s - m) * cexp)
    m_ref[...] = m
    l_ref[...] = jnp.sum(p, axis=1, keepdims=True)
    acc_ref[...] = jnp.dot(p.astype(BF16), v, preferred_element_type=F32)

    def body(kt, carry):
        s, v = scores(kt)
        m_old = m_ref[...]
        m_new = jnp.maximum(m_old, jnp.max(s, axis=1, keepdims=True))
        alpha = jnp.exp2((m_old - m_new) * cexp)
        p = jnp.exp2((s - m_new) * cexp)
        m_ref[...] = m_new
        l_ref[...] = l_ref[...] * alpha + jnp.sum(p, axis=1, keepdims=True)
        acc_ref[...] = acc_ref[...] * alpha + jnp.dot(p.astype(BF16), v, preferred_element_type=F32)
        return carry

    lax.fori_loop(0, kd, body, 0)
    o = (acc_ref[...] / l_ref[...]).astype(BF16)
    o_ref[0] = o.reshape(MLA_HEADS, MLA_TQ, KV_RANK)


def _mla_prompt(qcat, kcat, batch, seq):
    assert MLA_TQ == LANE and seq % MLA_TK == 0 and MLA_TK % MLA_TQ == 0
    nq = seq // MLA_TQ
    rows = MLA_HEADS * MLA_TQ
    return pl.pallas_call(
        _mla_prompt_kernel,
        out_shape=jax.ShapeDtypeStruct((batch * nq, MLA_HEADS, MLA_TQ, KV_RANK), BF16),
        grid=(batch, nq),
        in_specs=[pl.BlockSpec((1, MLA_HEADS, MLA_TQ, 2 * LANE), lambda b, i: (b * nq + i, 0, 0, 0)),
                  pl.BlockSpec((seq, 2 * LANE), lambda b, i: (b, 0))],
        out_specs=pl.BlockSpec((1, MLA_HEADS, MLA_TQ, KV_RANK), lambda b, i: (b * nq + i, 0, 0, 0)),
        scratch_shapes=[pltpu.VMEM((rows, 1), F32), pltpu.VMEM((rows, 1), F32), pltpu.VMEM((rows, KV_RANK), F32)],
        compiler_params=_cparams(("arbitrary", "arbitrary")),
        name="mla_prompt",
    )(qcat, kcat)


def _class_reduce(v, op):
    for sh in (8, 16, 32, 64):
        v = op(v, pltpu.roll(v, sh, axis=1))
    return v


def _moba_select_kernel(qm_ref, kmbd_ref, perm_ref, qaug_ref, *, blocks_per_seq):
    tm = qm_ref.shape[0]
    qm = qm_ref[...]
    gate = jnp.dot(qm, kmbd_ref[...], precision=lax.Precision.HIGHEST, preferred_element_type=F32)
    lane = lax.broadcasted_iota(jnp.int32, (tm, LANE), 1)
    nblk = lane // MOBA_HEADS
    own = pl.program_id(0) % blocks_per_seq
    past = nblk < own
    g = jnp.where(past, gate, -jnp.inf)
    picked = jnp.zeros((tm, LANE), F32)
    for _ in range(MOBA_TOPK):
        top = _class_reduce(g, jnp.maximum)
        first = _class_reduce(jnp.where(g == top, nblk, LANE), jnp.minimum)
        hit = nblk == first
        picked = picked + jnp.where(hit & (top > -jnp.inf), 1.0, 0.0)
        g = jnp.where(hit, -jnp.inf, g)
    masked = jnp.where(past & (picked < 0.5), 1.0, 0.0)
    bias = jnp.dot(masked.astype(BF16), perm_ref[...], preferred_element_type=F32) * NEG
    for gp in range(MOBA_HEADS // 2):
        ch = qm[:, gp * LANE:(gp + 1) * LANE]
        for e, src in ((0, ch), (1, pltpu.roll(ch, MOBA_DIM, axis=1))):
            hd = 2 * gp + e
            slot = jnp.where(lane < MOBA_DIM, src, 0.0) + bias[:, hd * LANE:(hd + 1) * LANE]
            qaug_ref[:, hd * LANE:(hd + 1) * LANE] = slot.astype(BF16)


def _moba_select(qm, kmean, batch, seq):
    n = qm.shape[0]
    tm = ROW_TILE
    nblk = seq // MOBA_BLOCK
    assert tm == MOBA_BLOCK and nblk * MOBA_HEADS <= LANE
    km4 = kmean.reshape(batch, nblk, MOBA_HEADS, MOBA_DIM)
    eye = jnp.eye(MOBA_HEADS, dtype=F32)
    kmbd = jnp.einsum("bnhd,hk->bhdnk", km4, eye).reshape(batch, MOBA_WIDTH, nblk * MOBA_HEADS)
    kmbd = jnp.pad(kmbd, ((0, 0), (0, 0), (0, LANE - nblk * MOBA_HEADS)))
    src = jnp.arange(LANE)
    dst = (src % MOBA_HEADS) * LANE + MOBA_DIM + src // MOBA_HEADS
    perm = (jnp.arange(MOBA_HEADS * LANE)[None, :] == dst[:, None]).astype(BF16)
    return pl.pallas_call(
        functools.partial(_moba_select_kernel, blocks_per_seq=nblk),
        out_shape=jax.ShapeDtypeStruct((n, MOBA_HEADS * LANE), BF16),
        grid=(n // tm,),
        in_specs=[pl.BlockSpec((tm, MOBA_WIDTH), lambda i: (i, 0)),
                  pl.BlockSpec((None, MOBA_WIDTH, LANE), lambda i: (i // nblk, 0, 0)),
                  pl.BlockSpec(perm.shape, lambda i: (0, 0))],
        out_specs=pl.BlockSpec((tm, MOBA_HEADS * LANE), lambda i: (i, 0)),
        compiler_params=_cparams(("arbitrary",)),
        name="moba_select",
    )(qm, kmbd, perm)


def _moba_prompt_kernel(qaug_ref, kaug_ref, v_ref, o_ref):
    own = pl.program_id(1)
    tq = qaug_ref.shape[0]
    cexp = MOBA_SCALE * LOG2E
    lane = lax.broadcasted_iota(jnp.int32, (tq, LANE), 1)
    tri = (lax.broadcasted_iota(jnp.int32, (tq, MOBA_BLOCK), 1)
           <= lax.broadcasted_iota(jnp.int32, (tq, MOBA_BLOCK), 0))
    for gp in range(MOBA_HEADS // 2):
        pair = []
        for e in range(2):
            hd = 2 * gp + e
            q = qaug_ref[:, hd * LANE:(hd + 1) * LANE]

            def block(n, hd=hd, gp=gp, q=q):
                r0 = pl.multiple_of(n * MOBA_BLOCK, MOBA_BLOCK)
                k = kaug_ref[pl.ds(r0, MOBA_BLOCK), hd * LANE:(hd + 1) * LANE]
                v = v_ref[pl.ds(r0, MOBA_BLOCK), gp * LANE:(gp + 1) * LANE]
                return _bdot_nt(q, k), v

            s, v = block(own)
            s = jnp.where(tri, s, NEG)
            m = jnp.max(s, axis=1, keepdims=True)
            p = jnp.exp2((s - m) * cexp)
            l = jnp.sum(p, axis=1, keepdims=True)
            acc = jnp.dot(p.astype(BF16), v, preferred_element_type=F32)

            def body(n, carry, block=block):
                m, l, acc = carry
                s, v = block(n)
                m_new = jnp.maximum(m, jnp.max(s, axis=1, keepdims=True))
                alpha = jnp.exp2((m - m_new) * cexp)
                p = jnp.exp2((s - m_new) * cexp)
                l = l * alpha + jnp.sum(p, axis=1, keepdims=True)
                acc = acc * alpha + jnp.dot(p.astype(BF16), v, preferred_element_type=F32)
                return m_new, l, acc

            m, l, acc = lax.fori_loop(0, own, body, (m, l, acc))
            pair.append(acc / l)
        o_ref[:, gp * LANE:(gp + 1) * LANE] = jnp.where(lane < MOBA_DIM, pair[0], pair[1]).astype(BF16)


def _moba_prompt(qaug, kaug, vmb, batch, seq):
    tq = MOBA_BLOCK
    nq = seq // tq
    return pl.pallas_call(
        _moba_prompt_kernel,
        out_shape=jax.ShapeDtypeStruct((batch * seq, MOBA_WIDTH), BF16),
        grid=(batch, nq),
        in_specs=[pl.BlockSpec((tq, MOBA_HEADS * LANE), lambda b, i: (b * nq + i, 0)),
                  pl.BlockSpec((seq, MOBA_HEADS * LANE), lambda b, i: (b, 0)),
                  pl.BlockSpec((seq, MOBA_WIDTH), lambda b, i: (b, 0))],
        out_specs=pl.BlockSpec((tq, MOBA_WIDTH), lambda b, i: (b * nq + i, 0)),
        compiler_params=_cparams(("arbitrary", "arbitrary")),
        name="moba_prompt",
    )(qaug, kaug, vmb)


def _attn_out_kernel(x_ref, omla_ref, omoba_ref, wv_ref, wout_ref, g_ref, b_ref, y_ref):
    tm = x_ref.shape[0]
    x = x_ref[...]
    mix = _bdot(omoba_ref[...], wout_ref[MLA_HEADS * V_DIM:, :])
    for gp in range(MLA_HEADS // 2):
        v2 = jnp.zeros((tm, LANE), F32)
        for e in range(2):
            hd = 2 * gp + e
            o = omla_ref[:, hd].reshape(tm, KV_RANK)
            v2 = v2 + jnp.dot(o, wv_ref[hd], preferred_element_type=F32)
        mix = mix + _bdot(v2, wout_ref[gp * LANE:(gp + 1) * LANE, :])
    y_ref[...] = _layer_norm(ALPHA * x + mix, g_ref[...], b_ref[...])


def _attn_out(x, omla, omoba, w, g, b):
    n = x.shape[0]
    tm = ROW_TILE
    nsub = tm // LANE
    full = lambda a: pl.BlockSpec(a.shape, lambda i: (0,) * a.ndim)
    return pl.pallas_call(
        _attn_out_kernel,
        out_shape=jax.ShapeDtypeStruct((n, D_MODEL), F32),
        grid=(n // tm,),
        in_specs=[pl.BlockSpec((tm, D_MODEL), lambda i: (i, 0)),
                  pl.BlockSpec((nsub, MLA_HEADS, LANE, KV_RANK), lambda i: (i, 0, 0, 0)),
                  pl.BlockSpec((tm, MOBA_WIDTH), lambda i: (i, 0)),
                  full(w["wv"]), full(w["w_out"]), full(g), full(b)],
        out_specs=pl.BlockSpec((tm, D_MODEL), lambda i: (i, 0)),
        compiler_params=_cparams(("arbitrary",)),
        name="attn_out",
    )(x, omla, omoba, w["wv"], w["w_out"], g, b)


def _mla_sample_kernel(pt_ref, qlat_ref, qrope_ref, latn_ref, ropen_ref, *rest, n_new):
    pps = PAGES_PER_STEP
    lat_refs, rope_refs = rest[:pps], rest[pps:2 * pps]
    o_ref, m_ref, l_ref, acc_ref = rest[2 * pps:]
    j = pl.program_id(1)
    cexp = MLA_SCALE * LOG2E
    ql = qlat_ref[0]
    qr = qrope_ref[0]
    rows = ql.shape[0]

    @pl.when(j == 0)
    def _():
        m_ref[...] = jnp.full((rows, 1), NEG, F32)
        l_ref[...] = jnp.zeros((rows, 1), F32)
        acc_ref[...] = jnp.zeros((rows, KV_RANK), F32)

    lat = jnp.concatenate([r[...] for r in lat_refs], axis=0).astype(BF16)
    rp = jnp.concatenate([r[...] for r in rope_refs], axis=0).astype(BF16)
    s = _bdot_nt(ql, lat) + _bdot_nt(qr, rp)
    m_old = m_ref[...]
    m_new = jnp.maximum(m_old, jnp.max(s, axis=1, keepdims=True))
    alpha = jnp.exp2((m_old - m_new) * cexp)
    p = jnp.exp2((s - m_new) * cexp)
    m_ref[...] = m_new
    l_ref[...] = l_ref[...] * alpha + jnp.sum(p, axis=1, keepdims=True)
    acc_ref[...] = acc_ref[...] * alpha + jnp.dot(p.astype(BF16), lat, preferred_element_type=F32)

    @pl.when(j == pl.num_programs(1) - 1)
    def _():
        latn = latn_ref[0].astype(BF16)
        s = _bdot_nt(ql, latn) + _bdot_nt(qr, ropen_ref[0])
        tq = lax.broadcasted_iota(jnp.int32, s.shape, 0) // MLA_HEADS
        tk = lax.broadcasted_iota(jnp.int32, s.shape, 1)
        s = jnp.where(tk <= tq, s, NEG)
        m_old = m_ref[...]
        m_new = jnp.maximum(m_old, jnp.max(s, axis=1, keepdims=True))
        alpha = jnp.exp2((m_old - m_new) * cexp)
        p = jnp.exp2((s - m_new) * cexp)
        l = l_ref[...] * alpha + jnp.sum(p, axis=1, keepdims=True)
        acc = acc_ref[...] * alpha + jnp.dot(p.astype(BF16), latn, preferred_element_type=F32)
        o_ref[0] = (acc / l).astype(BF16)


def _mla_sample(page_table, qlat, qrope, lat_new, rope_new, pool_lat, pool_rope, layer):
    nb, rows, _ = qlat.shape
    n_new = lat_new.shape[1]
    n_pages = page_table.shape[1]
    pps = PAGES_PER_STEP
    assert n_pages % pps == 0
    pt = page_table.reshape(-1)

    def page(width, i):
        return pl.BlockSpec((None, None, PAGE_SIZE, width),
                            lambda b, j, pt_ref: (layer, pt_ref[b * n_pages + j * pps + i], 0, 0))

    per_seq = lambda a: pl.BlockSpec((1,) + a.shape[1:], lambda b, j, pt_ref: (b, 0, 0))
    grid_spec = pltpu.PrefetchScalarGridSpec(
        num_scalar_prefetch=1,
        grid=(nb, n_pages // pps),
        in_specs=[per_seq(qlat), per_seq(qrope), per_seq(lat_new), per_seq(rope_new)]
                 + [page(KV_RANK, i) for i in range(pps)] + [page(ROPE_DIM, i) for i in range(pps)],
        out_specs=pl.BlockSpec((1, rows, KV_RANK), lambda b, j, pt_ref: (b, 0, 0)),
        scratch_shapes=[pltpu.VMEM((rows, 1), F32), pltpu.VMEM((rows, 1), F32), pltpu.VMEM((rows, KV_RANK), F32)],
    )
    return pl.pallas_call(
        functools.partial(_mla_sample_kernel, n_new=n_new),
        out_shape=jax.ShapeDtypeStruct((nb, rows, KV_RANK), BF16),
        grid_spec=grid_spec,
        compiler_params=_cparams(("arbitrary", "arbitrary")),
        name="mla_sample",
    )(pt, qlat, qrope, lat_new, rope_new, *([pool_lat] * pps), *([pool_rope] * pps))


def _moba_sample_kernel(pt_ref, qbd_ref, qbdf_ref, kn_ref, vn_ref, *rest, n_new, n_past_blocks):
    pps = PAGES_PER_STEP
    k_refs, v_refs = rest[:pps], rest[pps:2 * pps]
    o_ref, s_ref, ksum_ref, p_ref, pn_ref, den_ref, acc_ref = rest[2 * pps:]
    phase = pl.program_id(1)
    j = pl.program_id(2)
    cexp = MOBA_SCALE * LOG2E
    qbd = qbd_ref[0]
    rows = qbd.shape[0]
    past_len = n_past_blocks * MOBA_BLOCK
    pages_per_block = MOBA_BLOCK // PAGE_SIZE

    @pl.when((phase == 0) & (j == 0))
    def _():
        ksum_ref[...] = jnp.zeros(ksum_ref.shape, F32)

    @pl.when(phase == 0)
    def _():
        for i in range(pps):
            kp = k_refs[i][...]
            page = j * pps + i
            col = pl.multiple_of(page * PAGE_SIZE, PAGE_SIZE)
            s_ref[:, pl.ds(col, PAGE_SIZE)] = _bdot_nt(qbd, kp)
            blk = page // pages_per_block
            ksum_ref[pl.ds(blk, 1), :] = ksum_ref[pl.ds(blk, 1), :] + jnp.sum(kp, axis=0, keepdims=True)

    @pl.when((phase == 1) & (j == 0))
    def _():
        kmean = ksum_ref[...] * (1.0 / MOBA_BLOCK)
        gate = lax.dot_general(qbdf_ref[0], kmean, (((1,), (1,)), ((), ())),
                               precision=lax.Precision.HIGHEST, preferred_element_type=F32)
        nidx = lax.broadcasted_iota(jnp.int32, gate.shape, 1).astype(F32)
        g = gate
        picked = jnp.zeros(gate.shape, F32)
        for _ in range(min(MOBA_TOPK, n_past_blocks)):
            top = jnp.max(g, axis=1, keepdims=True)
            first = jnp.min(jnp.where(g == top, nidx, float(n_past_blocks)), axis=1, keepdims=True)
            hit = nidx == first
            picked = picked + jnp.where(hit, 1.0, 0.0)
            g = jnp.where(hit, -jnp.inf, g)
        blk_of_col = lax.broadcasted_iota(jnp.int32, (n_past_blocks, past_len), 1) // MOBA_BLOCK
        expand = jnp.where(blk_of_col == lax.broadcasted_iota(jnp.int32, (n_past_blocks, past_len), 0), 1.0, 0.0)
        chosen = jnp.dot(picked.astype(BF16), expand.astype(BF16), preferred_element_type=F32)
        s = jnp.where(chosen > 0.5, s_ref[...], NEG)
        sn = _bdot_nt(qbd, kn_ref[0])
        tq = lax.broadcasted_iota(jnp.int32, sn.shape, 0) // MOBA_HEADS
        tk = lax.broadcasted_iota(jnp.int32, sn.shape, 1)
        sn = jnp.where(tk <= tq, sn, NEG)
        m = jnp.maximum(jnp.max(s, axis=1, keepdims=True), jnp.max(sn, axis=1, keepdims=True))
        p = jnp.exp2((s - m) * cexp)
        pn = jnp.exp2((sn - m) * cexp)
        den_ref[...] = jnp.sum(p, axis=1, keepdims=True) + jnp.sum(pn, axis=1, keepdims=True)
        p_ref[...] = p.astype(BF16)
        pn_ref[...] = pn
        acc_ref[...] = jnp.zeros(acc_ref.shape, F32)

    @pl.when(phase == 1)
    def _():
        acc = acc_ref[...]
        for i in range(pps):
            col = pl.multiple_of((j * pps + i) * PAGE_SIZE, PAGE_SIZE)
            acc = acc + jnp.dot(p_ref[:, pl.ds(col, PAGE_SIZE)], v_refs[i][...].astype(BF16),
                                preferred_element_type=F32)
        acc_ref[...] = acc

    @pl.when((phase == 1) & (j == pl.num_programs(2) - 1))
    def _():
        o = (acc_ref[...] + _bdot(pn_ref[...], vn_ref[0])) / den_ref[...]
        hrow = lax.broadcasted_iota(jnp.int32, o.shape, 0) % MOBA_HEADS
        hcol = lax.broadcasted_iota(jnp.int32, o.shape, 1) // MOBA_DIM
        o = jnp.where(hrow == hcol, o, 0.0)
        for t in range(n_new):
            o_ref[0, t:t + 1, :] = jnp.sum(o[t * MOBA_HEADS:(t + 1) * MOBA_HEADS], axis=0,
                                           keepdims=True).astype(BF16)


def _moba_sample(page_table, qbd, qbdf, k_new, v_new, pool_k, pool_v, layer):
    nb, rows, _ = qbd.shape
    n_new = k_new.shape[1]
    n_pages = page_table.shape[1]
    pps = PAGES_PER_STEP
    past_len = n_pages * PAGE_SIZE
    assert n_pages % pps == 0 and past_len % MOBA_BLOCK == 0 and n_new <= MOBA_BLOCK
    n_past_blocks = past_len // MOBA_BLOCK
    steps = n_pages // pps
    pt = page_table.reshape(-1)
    pool_k = pool_k.reshape(pool_k.shape[0], pool_k.shape[1], PAGE_SIZE, MOBA_WIDTH)
    pool_v = pool_v.reshape(pool_v.shape[0], pool_v.shape[1], PAGE_SIZE, MOBA_WIDTH)

    def kpage(i):
        return pl.BlockSpec((None, None, PAGE_SIZE, MOBA_WIDTH),
                            lambda b, ph, j, pt_ref: (layer, pt_ref[b * n_pages + (j * (1 - ph) + (steps - 1) * ph) * pps + i], 0, 0))

    def vpage(i):
        return pl.BlockSpec((None, None, PAGE_SIZE, MOBA_WIDTH),
                            lambda b, ph, j, pt_ref: (layer, pt_ref[b * n_pages + j * ph * pps + i], 0, 0))

    per_seq = lambda a: pl.BlockSpec((1,) + a.shape[1:], lambda b, ph, j, pt_ref: (b, 0, 0))
    grid_spec = pltpu.PrefetchScalarGridSpec(
        num_scalar_prefetch=1,
        grid=(nb, 2, steps),
        in_specs=[per_seq(qbd), per_seq(qbdf), per_seq(k_new), per_seq(v_new)]
                 + [kpage(i) for i in range(pps)] + [vpage(i) for i in range(pps)],
        out_specs=pl.BlockSpec((1, n_new, MOBA_WIDTH), lambda b, ph, j, pt_ref: (b, 0, 0)),
        scratch_shapes=[pltpu.VMEM((rows, past_len), F32),
                        pltpu.VMEM((n_past_blocks, MOBA_WIDTH), F32),
                        pltpu.VMEM((rows, past_len), BF16),
                        pltpu.VMEM((rows, n_new), F32),
                        pltpu.VMEM((rows, 1), F32),
                        pltpu.VMEM((rows, MOBA_WIDTH), F32)],
    )
    return pl.pallas_call(
        functools.partial(_moba_sample_kernel, n_new=n_new, n_past_blocks=n_past_blocks),
        out_shape=jax.ShapeDtypeStruct((nb, n_new, MOBA_WIDTH), BF16),
        grid_spec=grid_spec,
        compiler_params=_cparams(("arbitrary", "arbitrary", "arbitrary")),
        name="moba_sample",
    )(pt, qbd, qbdf, k_new, v_new, *([pool_k] * pps), *([pool_v] * pps))


def _s5_param_kernel(are_c_ref, aim_c_ref, dt_c_ref, bre_ref, bim_ref, are_r_ref, aim_r_ref, dt_r_ref,
                     bbre_ref, bbim_ref, pw_ref, step_ref):
    def abar(are, aim, dt):
        mag = jnp.exp(are * dt)
        return mag * jnp.cos(aim * dt), mag * jnp.sin(aim * dt)

    are, aim = are_c_ref[...], aim_c_ref[...]
    ar, ai = abar(are, aim, dt_c_ref[...])
    nr, ni = ar - 1.0, ai
    den = are * are + aim * aim
    cr = (nr * are + ni * aim) / den
    ci = (ni * are - nr * aim) / den
    bre, bim = bre_ref[...], bim_ref[...]
    bbre_ref[...] = cr * bre - ci * bim
    bbim_ref[...] = cr * bim + ci * bre
    ar, ai = abar(are_r_ref[...], aim_r_ref[...], dt_r_ref[...])
    pr, pi = ar, ai
    powers = []
    for i in range(SUBLANE):
        pw_ref[0, i:i + 1, :] = pr
        pw_ref[1, i:i + 1, :] = pi
        powers.append((pr, pi))
        pr, pi = pr * ar - pi * ai, pr * ai + pi * ar
    zero = jnp.zeros_like(ar)
    for k in range(3):
        d = 1 << k
        for i in range(SUBLANE):
            step_ref[2 * k, i:i + 1, :] = powers[d - 1][0] if i >= d else zero
            step_ref[2 * k + 1, i:i + 1, :] = powers[d - 1][1] if i >= d else zero


def _s5_params(a_re, a_im, log_dt, b_re, b_im, c_re, c_im):
    col = lambda a: a.reshape(SSM_CH, 1)
    rowv = lambda a: a.reshape(1, SSM_CH)
    dt = jnp.exp(log_dt.astype(F32))
    dt_full = jnp.broadcast_to(dt[:, None], (SSM_GROUPS, SSM_STATE))
    vm = lambda shape: pl.BlockSpec(shape, lambda: (0,) * len(shape))
    ins = [col(a_re), col(a_im), col(dt_full), b_re.reshape(SSM_CH, SSM_GROUP), b_im.reshape(SSM_CH, SSM_GROUP),
           rowv(a_re), rowv(a_im), rowv(dt_full)]
    bbre, bbim, pw, step = pl.pallas_call(
        _s5_param_kernel,
        out_shape=[jax.ShapeDtypeStruct((SSM_CH, SSM_GROUP), F32), jax.ShapeDtypeStruct((SSM_CH, SSM_GROUP), F32),
                   jax.ShapeDtypeStruct((2, SUBLANE, SSM_CH), F32), jax.ShapeDtypeStruct((6, SUBLANE, SSM_CH), F32)],
        in_specs=[vm(a.shape) for a in ins],
        out_specs=[vm((SSM_CH, SSM_GROUP)), vm((SSM_CH, SSM_GROUP)), vm((2, SUBLANE, SSM_CH)), vm((6, SUBLANE, SSM_CH))],
        compiler_params=pltpu.CompilerParams(vmem_limit_bytes=VMEM_LIMIT),
        name="s5_params",
    )(*ins)
    gpc = SSM_GROUPS // S5_CHUNKS
    eye = jnp.eye(gpc, dtype=F32)

    def in_mat(bb):
        b4 = bb.reshape(S5_CHUNKS, gpc, SSM_STATE, SSM_GROUP)
        return jnp.einsum("kgpc,gh->kgchp", b4, eye).reshape(S5_CHUNKS, S5_CIN, S5_CST)

    def out_mat(cc):
        c4 = cc.reshape(S5_CHUNKS, gpc, SSM_GROUP, SSM_STATE)
        return jnp.einsum("kgcp,gh->kgphc", c4, eye).reshape(S5_CHUNKS, S5_CST, S5_CIN)

    wb = jnp.concatenate([in_mat(bbre), in_mat(bbim)], axis=-1).astype(BF16)
    return dict(wb=wb, wc_re=out_mat(c_re.astype(F32)).astype(BF16), wc_im=out_mat(c_im.astype(F32)).astype(BF16),
                pw=pw, step=step)


def _s5_tail(x, y, d_ref, wglu_ref, g_ref, b_ref):
    z = _gelu(y + d_ref[...] * x)
    zz = _bdot(z, wglu_ref[...])
    out = zz[:, :D_MODEL] * _sigmoid(zz[:, D_MODEL:])
    return _layer_norm(ALPHA * x + out, g_ref[...], b_ref[...])


def _s5_prompt_kernel(x_ref, wb_ref, wcre_ref, wcim_ref, pw_ref, step_ref, d_ref, wglu_ref, g_ref, b_ref,
                      y_ref, hl_ref, carry_ref, h_scr):
    tt = x_ref.shape[0]

    @pl.when(pl.program_id(1) == 0)
    def _():
        carry_ref[...] = jnp.zeros(carry_ref.shape, F32)

    x = x_ref[...]
    ys = []
    for k in range(S5_CHUNKS):
        lo = k * S5_CST
        h_scr[...] = _bdot(x[:, k * S5_CIN:(k + 1) * S5_CIN], wb_ref[k])
        pr, pi = pw_ref[0, :, lo:lo + S5_CST], pw_ref[1, :, lo:lo + S5_CST]
        steps = [(step_ref[2 * s, :, lo:lo + S5_CST], step_ref[2 * s + 1, :, lo:lo + S5_CST], 1 << s)
                 for s in range(3)]

        def block(i, carry, pr=pr, pi=pi, steps=steps):
            cr, ci = carry
            r0 = pl.multiple_of(i * SUBLANE, SUBLANE)
            hr = h_scr[pl.ds(r0, SUBLANE), 0:S5_CST]
            hi = h_scr[pl.ds(r0, SUBLANE), S5_CST:2 * S5_CST]
            for sr, si, sh in steps:
                rr, ri = pltpu.roll(hr, sh, axis=0), pltpu.roll(hi, sh, axis=0)
                hr, hi = hr + sr * rr - si * ri, hi + sr * ri + si * rr
            hr, hi = hr + pr * cr - pi * ci, hi + pr * ci + pi * cr
            h_scr[pl.ds(r0, SUBLANE), 0:S5_CST] = hr
            h_scr[pl.ds(r0, SUBLANE), S5_CST:2 * S5_CST] = hi
            return hr[SUBLANE - 1:SUBLANE], hi[SUBLANE - 1:SUBLANE]

        cr, ci = lax.fori_loop(0, tt // SUBLANE, block,
                               (carry_ref[k, 0:1, 0:S5_CST], carry_ref[k, 0:1, S5_CST:2 * S5_CST]))
        carry_ref[k, 0:1, 0:S5_CST] = cr
        carry_ref[k, 0:1, S5_CST:2 * S5_CST] = ci
        ys.append(_bdot(h_scr[:, 0:S5_CST], wcre_ref[k]) - _bdot(h_scr[:, S5_CST:2 * S5_CST], wcim_ref[k]))
    y_ref[...] = _s5_tail(x, jnp.concatenate(ys, axis=-1), d_ref, wglu_ref, g_ref, b_ref)
    hl_ref[0] = carry_ref[...]


def _s5_prompt(x, sp, d, wglu, g, b, batch, seq):
    tt = ROW_TILE
    nt = seq // tt
    full = lambda a: pl.BlockSpec(a.shape, lambda bb, i: (0,) * a.ndim)
    y, hl = pl.pallas_call(
        _s5_prompt_kernel,
        out_shape=[jax.ShapeDtypeStruct((batch * seq, D_MODEL), F32),
                   jax.ShapeDtypeStruct((batch, S5_CHUNKS, SUBLANE, 2 * S5_CST), F32)],
        grid=(batch, nt),
        in_specs=[pl.BlockSpec((tt, D_MODEL), lambda bb, i: (bb * nt + i, 0)),
                  full(sp["wb"]), full(sp["wc_re"]), full(sp["wc_im"]), full(sp["pw"]), full(sp["step"]),
                  full(d), full(wglu), full(g), full(b)],
        out_specs=[pl.BlockSpec((tt, D_MODEL), lambda bb, i: (bb * nt + i, 0)),
                   pl.BlockSpec((1, S5_CHUNKS, SUBLANE, 2 * S5_CST), lambda bb, i: (bb, 0, 0, 0))],
        scratch_shapes=[pltpu.VMEM((S5_CHUNKS, SUBLANE, 2 * S5_CST), F32), pltpu.VMEM((tt, 2 * S5_CST), F32)],
        compiler_params=_cparams(("arbitrary", "arbitrary")),
        name="s5_prompt",
    )(x, sp["wb"], sp["wc_re"], sp["wc_im"], sp["pw"], sp["step"], d, wglu, g, b)
    h_re = hl[:, :, 0, :S5_CST].reshape(batch, SSM_GROUPS, SSM_STATE)
    h_im = hl[:, :, 0, S5_CST:].reshape(batch, SSM_GROUPS, SSM_STATE)
    return y, h_re, h_im


def _s5_sample_kernel(x_ref, h0re_ref, h0im_ref, wb_ref, wcre_ref, wcim_ref, pw_ref, d_ref, wglu_ref, g_ref, b_ref,
                      y_ref, hre_ref, him_ref):
    @pl.when(pl.program_id(0) == 0)
    def _():
        hre_ref[...] = h0re_ref[...]
        him_ref[...] = h0im_ref[...]

    x = x_ref[...]
    ys = []
    for k in range(S5_CHUNKS):
        lo = k * S5_CST
        bu = _bdot(x[:, k * S5_CIN:(k + 1) * S5_CIN], wb_ref[k])
        ar, ai = pw_ref[0, 0:1, lo:lo + S5_CST], pw_ref[1, 0:1, lo:lo + S5_CST]
        hr, hi = hre_ref[:, lo:lo + S5_CST], him_ref[:, lo:lo + S5_CST]
        nr = ar * hr - ai * hi + bu[:, :S5_CST]
        ni = ar * hi + ai * hr + bu[:, S5_CST:]
        hre_ref[:, lo:lo + S5_CST] = nr
        him_ref[:, lo:lo + S5_CST] = ni
        ys.append(_bdot(nr, wcre_ref[k]) - _bdot(ni, wcim_ref[k]))
    y_ref[...] = _s5_tail(x, jnp.concatenate(ys, axis=-1), d_ref, wglu_ref, g_ref, b_ref)


def _s5_sample(x, h0_re, h0_im, sp, d, wglu, g, b, nb, n_new):
    full = lambda a: pl.BlockSpec(a.shape, lambda t: (0,) * a.ndim)
    state = pl.BlockSpec((nb, SSM_CH), lambda t: (0, 0))
    y, h_re, h_im = pl.pallas_call(
        _s5_sample_kernel,
        out_shape=[jax.ShapeDtypeStruct((n_new * nb, D_MODEL), F32),
                   jax.ShapeDtypeStruct((nb, SSM_CH), F32), jax.ShapeDtypeStruct((nb, SSM_CH), F32)],
        grid=(n_new,),
        in_specs=[pl.BlockSpec((nb, D_MODEL), lambda t: (t, 0)), state, state,
                  full(sp["wb"]), full(sp["wc_re"]), full(sp["wc_im"]), full(sp["pw"]),
                  full(d), full(wglu), full(g), full(b)],
        out_specs=[pl.BlockSpec((nb, D_MODEL), lambda t: (t, 0)), state, state],
        compiler_params=_cparams(("arbitrary",)),
        name="s5_sample",
    )(x, h0_re.reshape(nb, SSM_CH), h0_im.reshape(nb, SSM_CH), sp["wb"], sp["wc_re"], sp["wc_im"], sp["pw"],
      d, wglu, g, b)
    return y, h_re.reshape(nb, SSM_GROUPS, SSM_STATE), h_im.reshape(nb, SSM_GROUPS, SSM_STATE)


def _ffn_prompt_kernel(x_ref, wup_ref, cw_ref, cb_ref, wdn_ref, g_ref, b_ref, y_ref, st_ref,
                       carry_ref, ua_ref, ub_ref):
    tm = x_ref.shape[0]

    @pl.when(pl.program_id(1) == 0)
    def _():
        carry_ref[...] = jnp.zeros(carry_ref.shape, F32)

    x = x_ref[...]
    xb = x.astype(BF16)

    def conv(col0, scr):
        cols = slice(col0, col0 + FFN_CW)
        u = jnp.dot(xb, wup_ref[:, cols], preferred_element_type=F32)
        scr[0:SUBLANE, :] = carry_ref[:, cols]
        scr[SUBLANE:SUBLANE + tm, :] = u
        carry_ref[:, cols] = u[tm - SUBLANE:tm]
        p1 = scr[SUBLANE - 1:SUBLANE - 1 + tm, :]
        p2 = scr[SUBLANE - 2:SUBLANE - 2 + tm, :]
        return cb_ref[:, cols] + cw_ref[0:1, cols] * p2 + cw_ref[1:2, cols] * p1 + cw_ref[2:3, cols] * u

    acc = jnp.zeros((tm, D_MODEL), F32)
    for c in range(D_FF // FFN_CW):
        act = _gelu(conv(c * FFN_CW, ua_ref)) * conv(D_FF + c * FFN_CW, ub_ref)
        acc = acc + _bdot(act, wdn_ref[c * FFN_CW:(c + 1) * FFN_CW, :])
    y_ref[...] = _layer_norm(ALPHA * x + acc, g_ref[...], b_ref[...])
    st_ref[0] = carry_ref[...]


def _ffn_prompt(x, w, g, b, batch, seq):
    tm = ROW_TILE
    nt = seq // tm
    assert CONV_W - 1 <= SUBLANE and D_FF % FFN_CW == 0
    full = lambda a: pl.BlockSpec(a.shape, lambda bb, i: (0,) * a.ndim)
    y, st = pl.pallas_call(
        _ffn_prompt_kernel,
        out_shape=[jax.ShapeDtypeStruct((batch * seq, D_MODEL), F32),
                   jax.ShapeDtypeStruct((batch, SUBLANE, 2 * D_FF), F32)],
        grid=(batch, nt),
        in_specs=[pl.BlockSpec((tm, D_MODEL), lambda bb, i: (bb * nt + i, 0)),
                  full(w["w_up"]), full(w["conv_w"]), full(w["conv_b"]), full(w["w_down"]), full(g), full(b)],
        out_specs=[pl.BlockSpec((tm, D_MODEL), lambda bb, i: (bb * nt + i, 0)),
                   pl.BlockSpec((1, SUBLANE, 2 * D_FF), lambda bb, i: (bb, 0, 0))],
        scratch_shapes=[pltpu.VMEM((SUBLANE, 2 * D_FF), F32),
                        pltpu.VMEM((SUBLANE + tm, FFN_CW), F32), pltpu.VMEM((SUBLANE + tm, FFN_CW), F32)],
        compiler_params=_cparams(("arbitrary", "arbitrary")),
        name="ffn_prompt",
    )(x, w["w_up"], w["conv_w"], w["conv_b"], w["w_down"], g, b)
    return y, st[:, SUBLANE - (CONV_W - 1):, :]


def _ffn_sample_kernel(x_ref, st_ref, wup_ref, cw_ref, cb_ref, wdn_ref, g_ref, b_ref, y_ref, ns_ref):
    tm = x_ref.shape[0]
    w2 = 2 * D_FF

    @pl.when(pl.program_id(0) == 0)
    def _():
        ns_ref[...] = st_ref[...]

    x = x_ref[...]
    xb = x.astype(BF16)

    def conv(col0):
        cols = slice(col0, col0 + FFN_CW)
        cols1 = slice(w2 + col0, w2 + col0 + FFN_CW)
        u = jnp.dot(xb, wup_ref[:, cols], preferred_element_type=F32)
        p2 = ns_ref[:, cols]
        p1 = ns_ref[:, cols1]
        ns_ref[:, cols] = p1
        ns_ref[:, cols1] = u
        return cb_ref[:, cols] + cw_ref[0:1, cols] * p2 + cw_ref[1:2, cols] * p1 + cw_ref[2:3, cols] * u

    acc = jnp.zeros((tm, D_MODEL), F32)
    for c in range(D_FF // FFN_CW):
        act = _gelu(conv(c * FFN_CW)) * conv(D_FF + c * FFN_CW)
        acc = acc + _bdot(act, wdn_ref[c * FFN_CW:(c + 1) * FFN_CW, :])
    y_ref[...] = _layer_norm(ALPHA * x + acc, g_ref[...], b_ref[...])


def _ffn_sample(x, state, w, g, b, nb, n_new):
    assert CONV_W == 3
    full = lambda a: pl.BlockSpec(a.shape, lambda t: (0,) * a.ndim)
    st = state.reshape(nb, (CONV_W - 1) * 2 * D_FF)
    y, ns = pl.pallas_call(
        _ffn_sample_kernel,
        out_shape=[jax.ShapeDtypeStruct((n_new * nb, D_MODEL), F32), jax.ShapeDtypeStruct(st.shape, F32)],
        grid=(n_new,),
        in_specs=[pl.BlockSpec((nb, D_MODEL), lambda t: (t, 0)), full(st),
                  full(w["w_up"]), full(w["conv_w"]), full(w["conv_b"]), full(w["w_down"]), full(g), full(b)],
        out_specs=[pl.BlockSpec((nb, D_MODEL), lambda t: (t, 0)), full(st)],
        compiler_params=_cparams(("arbitrary",)),
        name="ffn_sample",
    )(x, st, w["w_up"], w["conv_w"], w["conv_b"], w["w_down"], g, b)
    return y, ns.reshape(nb, CONV_W - 1, 2 * D_FF)


def _attn_weights(w_in, q_norm, w_q_b, kv_norm, w_kv_b, w_out):
    o1 = Q_RANK
    o2 = o1 + KV_RANK
    o3 = o2 + ROPE_DIM
    kpe = w_in[:, o2:o3]
    w_in_r = jnp.concatenate([w_in[:, :o2], w_in[:, o3:]] + [kpe] * (LANE // ROPE_DIM), axis=1).astype(BF16)
    assert w_in_r.shape[1] == W_IN_COLS
    qb = w_q_b.reshape(Q_RANK, MLA_HEADS, NOPE_DIM + ROPE_DIM)
    nope = jnp.pad(qb[:, :, :NOPE_DIM], ((0, 0), (0, 0), (0, LANE - NOPE_DIM))).reshape(Q_RANK, MLA_HEADS * LANE)
    ropew = qb[:, :, NOPE_DIM:].reshape(Q_RANK, MLA_HEADS * ROPE_DIM)
    w_q_b_r = jnp.concatenate([nope, ropew], axis=1).astype(BF16)
    wk = jnp.transpose(w_kv_b[:, :, :NOPE_DIM], (1, 2, 0))
    wk = jnp.pad(wk, ((0, 0), (0, LANE - NOPE_DIM), (0, 0))).astype(BF16)
    wv = jnp.transpose(w_kv_b[:, :, NOPE_DIM:], (1, 0, 2))
    wv_lo = jnp.pad(wv, ((0, 0), (0, 0), (0, LANE - V_DIM)))
    wv_hi = jnp.pad(wv, ((0, 0), (0, 0), (LANE - V_DIM, 0)))
    odd = (jnp.arange(MLA_HEADS) % 2 == 1)[:, None, None]
    wvp = jnp.where(odd, wv_hi, wv_lo).astype(BF16)
    return dict(w_in=w_in_r, q_norm=q_norm.reshape(1, Q_RANK), w_q_b=w_q_b_r, wk=wk,
                kv_norm=kv_norm.reshape(1, KV_RANK), wv=wvp, w_out=w_out.astype(BF16))


def _ffn_weights(w_up, conv_w, conv_b, w_down):
    return dict(w_up=w_up.astype(BF16), conv_w=conv_w, conv_b=conv_b.reshape(1, 2 * D_FF), w_down=w_down.astype(BF16))


def _to_time_major(a):
    return jnp.swapaxes(a, 0, 1).reshape((a.shape[0] * a.shape[1],) + a.shape[2:])


def _from_time_major(a, nb):
    return jnp.swapaxes(a.reshape((a.shape[0] // nb, nb) + a.shape[1:]), 0, 1)


def kernel(x_prompt, x_sample, cache_mla_latent, cache_mla_rope, cache_moba_k, cache_moba_v, state_s5_re, state_s5_im, state_ffn_conv, page_table, ln_mix_g, ln_mix_b, ln_ffn_g, ln_ffn_b, att_w_in, mla_q_norm, mla_w_q_b, mla_kv_norm, mla_w_kv_b, att_w_out, s5_a_re, s5_a_im, s5_log_dt, s5_b_re, s5_b_im, s5_c_re, s5_c_im, s5_d, s5_w_glu, ffn_w_up, ffn_conv_w, ffn_conv_b, ffn_w_down):
    batch, seq, _ = x_prompt.shape
    nb, n_new, _ = x_sample.shape
    past_len = page_table.shape[1] * PAGE_SIZE
    depth = ln_mix_g.shape[0]
    assert seq % MLA_TK == 0 and nb % SUBLANE == 0 and (nb * n_new) % ROW_TILE == 0

    xp = x_prompt.reshape(batch * seq, D_MODEL)
    xs = _to_time_major(x_sample)
    tabs_p = _rope_tables(seq, 0)
    tabs_s = [jnp.repeat(t[:n_new], nb, axis=0) for t in _rope_tables(SUBLANE * pl.cdiv(n_new, SUBLANE), past_len)]
    row = lambda v: v.reshape(1, -1)

    outs_p = {k: [] for k in ("lat", "rope", "mk", "mv", "sre", "sim", "conv")}
    outs_s = {k: [] for k in ("lat", "rope", "mk", "mv", "sre", "sim", "conv")}
    for l in range(depth):
        g_mix, b_mix = row(ln_mix_g[l]), row(ln_mix_b[l])
        if l % 2 == 0:
            a = l // 2
            w = _attn_weights(att_w_in[a], mla_q_norm[a], mla_w_q_b[a], mla_kv_norm[a], mla_w_kv_b[a], att_w_out[a])
            qcat, kcat, lat, rp, qm, km, kaug, vm, vmb, kmean = _attn_proj(xp, w, tabs_p, seq)
            o_mla = _mla_prompt(qcat, kcat, batch, seq)
            qaug = _moba_select(qm, kmean, batch, seq)
            o_moba = _moba_prompt(qaug, kaug, vmb, batch, seq)
            xp = _attn_out(xp, o_mla, o_moba, w, g_mix, b_mix)
            outs_p["lat"].append(lat.reshape(batch, seq, KV_RANK))
            outs_p["rope"].append(rp.reshape(batch, seq, ROPE_DIM))
            outs_p["mk"].append(km.reshape(batch, seq, MOBA_HEADS, MOBA_DIM))
            outs_p["mv"].append(vm.reshape(batch, seq, MOBA_HEADS, MOBA_DIM))
            qcat, _, lat, rp, qm, km, _, vm, _, _ = _attn_proj(xs, w, tabs_s, nb * n_new)
            lat_b, rp_b = _from_time_major(lat, nb), _from_time_major(rp, nb)
            km_b, vm_b = _from_time_major(km, nb), _from_time_major(vm, nb)
            qm_b = _from_time_major(qm, nb)
            qc = qcat.reshape(n_new, nb // LANE, MLA_HEADS, LANE, 2 * LANE)
            qc = jnp.transpose(qc, (1, 3, 0, 2, 4)).reshape(nb, n_new * MLA_HEADS, 2 * LANE)
            qlat = qc[:, :, :KV_RANK]
            qr4 = qc[:, :, KV_RANK:].reshape(nb, n_new, MLA_HEADS, LANE // ROPE_DIM, ROPE_DIM)
            grp = (jnp.arange(MLA_HEADS) % (LANE // ROPE_DIM))
            qrope = jnp.take_along_axis(qr4, grp[None, None, :, None, None], axis=3)[:, :, :, 0, :]
            qrope = qrope.reshape(nb, n_new * MLA_HEADS, ROPE_DIM)
            o_mla = _mla_sample(page_table, qlat, qrope, lat_b, rp_b.astype(BF16), cache_mla_latent, cache_mla_rope, a)
            q4 = qm_b.reshape(nb, n_new, MOBA_HEADS, MOBA_DIM)
            eye = jnp.eye(MOBA_HEADS, dtype=F32)
            qbdf = jnp.einsum("bthd,hk->bthkd", q4, eye).reshape(nb, n_new * MOBA_HEADS, MOBA_WIDTH)
            o_moba = _moba_sample(page_table, qbdf.astype(BF16), qbdf, km_b.astype(BF16), vm_b.astype(BF16),
                                  cache_moba_k, cache_moba_v, a)
            om = o_mla.reshape(nb // LANE, LANE, n_new, MLA_HEADS, KV_RANK)
            om = jnp.transpose(om, (2, 0, 3, 1, 4)).reshape(n_new * nb // LANE, MLA_HEADS, LANE, KV_RANK)
            xs = _attn_out(xs, om, _to_time_major(o_moba), w, g_mix, b_mix)
            outs_s["lat"].append(lat_b)
            outs_s["rope"].append(rp_b)
            outs_s["mk"].append(km_b.reshape(nb, n_new, MOBA_HEADS, MOBA_DIM))
            outs_s["mv"].append(vm_b.reshape(nb, n_new, MOBA_HEADS, MOBA_DIM))
        else:
            s = l // 2
            sp = _s5_params(s5_a_re[s], s5_a_im[s], s5_log_dt[s], s5_b_re[s], s5_b_im[s], s5_c_re[s], s5_c_im[s])
            d, wglu = row(s5_d[s]), s5_w_glu[s].astype(BF16)
            xp, hre, him = _s5_prompt(xp, sp, d, wglu, g_mix, b_mix, batch, seq)
            outs_p["sre"].append(hre)
            outs_p["sim"].append(him)
            xs, hre, him = _s5_sample(xs, state_s5_re[s], state_s5_im[s], sp, d, wglu, g_mix, b_mix, nb, n_new)
            outs_s["sre"].append(hre)
            outs_s["sim"].append(him)
        fw = _ffn_weights(ffn_w_up[l], ffn_conv_w[l], ffn_conv_b[l], ffn_w_down[l])
        g_ffn, b_ffn = row(ln_ffn_g[l]), row(ln_ffn_b[l])
        xp, cp = _ffn_prompt(xp, fw, g_ffn, b_ffn, batch, seq)
        xs, cs = _ffn_sample(xs, state_ffn_conv[l], fw, g_ffn, b_ffn, nb, n_new)
        outs_p["conv"].append(cp)
        outs_s["conv"].append(cs)

    keys = ("lat", "rope", "mk", "mv", "sre", "sim", "conv")
    return ((xp.reshape(batch, seq, D_MODEL), _from_time_major(xs, nb))
            + tuple(jnp.stack(outs_p[k]) for k in keys) + tuple(jnp.stack(outs_s[k]) for k in keys))
```

```python
import functools
import math

import jax
import jax.numpy as jnp
from jax import lax
from jax.experimental import pallas as pl
from jax.experimental.pallas import tpu as pltpu

F32 = jnp.float32
BF16 = jnp.bfloat16

D_MODEL = 1024
PAGE_SIZE = 128
MLA_HEADS = 8
Q_RANK = 256
KV_RANK = 128
NOPE_DIM = 64
ROPE_DIM = 32
V_DIM = 64
MLA_SCALE = (NOPE_DIM + ROPE_DIM) ** -0.5
MOBA_HEADS = 8
MOBA_DIM = 64
MOBA_WIDTH = MOBA_HEADS * MOBA_DIM
MOBA_BLOCK = 256
MOBA_TOPK = 3
MOBA_SCALE = MOBA_DIM ** -0.5
ROPE_THETA = 10000.0
SSM_GROUP = 16
SSM_GROUPS = D_MODEL // SSM_GROUP
SSM_STATE = 64
SSM_CH = SSM_GROUPS * SSM_STATE
D_FF = 2816
CONV_W = 3
DEPTH = 4
ALPHA = (2 * DEPTH) ** 0.25
LN_EPS = 1e-5
RMS_EPS = 1e-6

LANE = 128
SUBLANE = 8
NEG = -1e30
LOG2E = math.log2(math.e)
ROW_TILE = 256
MLA_TQ = 128
MLA_TK = 512
S5_CHUNKS = 4
S5_CIN = D_MODEL // S5_CHUNKS
S5_CST = SSM_CH // S5_CHUNKS
FFN_CW = 256
PAGES_PER_STEP = 16
W_IN_COLS = Q_RANK + KV_RANK + 3 * MOBA_WIDTH + LANE
VMEM_LIMIT = 56 * 1024 * 1024


def _cparams(sem):
    return pltpu.CompilerParams(dimension_semantics=sem, vmem_limit_bytes=VMEM_LIMIT)


def _bdot(a, b):
    return jnp.dot(a.astype(BF16), b.astype(BF16), preferred_element_type=F32)


def _bdot_nt(a, b):
    return lax.dot_general(a.astype(BF16), b.astype(BF16), (((1,), (1,)), ((), ())),
                           preferred_element_type=F32)


def _layer_norm(v, g, b):
    mu = jnp.mean(v, axis=-1, keepdims=True)
    d = v - mu
    var = jnp.mean(d * d, axis=-1, keepdims=True)
    return d * lax.rsqrt(var + LN_EPS) * g + b


def _rms_norm(v, g):
    return v * lax.rsqrt(jnp.mean(v * v, axis=-1, keepdims=True) + RMS_EPS) * g


def _gelu(v):
    return 0.5 * v * (1.0 + jnp.tanh(math.sqrt(2.0 / math.pi) * (v + 0.044715 * (v * v * v))))


def _sigmoid(v):
    return 1.0 / (1.0 + jnp.exp(-v))


def _swap_halves(v, half):
    w = v.shape[-1]
    lane = lax.broadcasted_iota(jnp.int32, v.shape, 1)
    fwd = pltpu.roll(v, w - half, axis=1)
    bwd = pltpu.roll(v, half, axis=1)
    return jnp.where((lane % (2 * half)) < half, fwd, bwd)


def _rope(v, cos, sin_signed, half):
    outs = []
    for c in range(v.shape[-1] // LANE):
        vc = v[:, c * LANE:(c + 1) * LANE]
        outs.append(vc * cos + _swap_halves(vc, half) * sin_signed)
    return outs[0] if len(outs) == 1 else jnp.concatenate(outs, axis=-1)


def _rope_table_kernel(inv64_ref, sg64_ref, inv32_ref, sg32_ref, c64_ref, s64_ref, c32_ref, s32_ref, *, offset):
    rows = c64_ref.shape[0]
    base = pl.program_id(0) * rows + offset
    pos = (base + lax.broadcasted_iota(jnp.int32, (rows, LANE), 0)).astype(F32)
    a64 = pos * inv64_ref[...]
    c64_ref[...] = jnp.cos(a64)
    s64_ref[...] = jnp.sin(a64) * sg64_ref[...]
    a32 = pos * inv32_ref[...]
    c32_ref[...] = jnp.cos(a32)
    s32_ref[...] = jnp.sin(a32) * sg32_ref[...]


def _rope_tables(n_rows, offset):
    def lane_consts(d):
        half = d // 2
        inv = ROPE_THETA ** (-jnp.arange(half, dtype=F32) * 2.0 / d)
        l = jnp.arange(LANE)
        return (inv[(l % d) % half].reshape(1, LANE),
                jnp.where((l % d) < half, -1.0, 1.0).astype(F32).reshape(1, LANE))
    inv64, sg64 = lane_consts(MOBA_DIM)
    inv32, sg32 = lane_consts(ROPE_DIM)
    rows = min(n_rows, 512)
    assert n_rows % rows == 0 and rows % SUBLANE == 0
    const = pl.BlockSpec((1, LANE), lambda i: (0, 0))
    out = pl.BlockSpec((rows, LANE), lambda i: (i, 0))
    return pl.pallas_call(
        functools.partial(_rope_table_kernel, offset=offset),
        out_shape=[jax.ShapeDtypeStruct((n_rows, LANE), F32)] * 4,
        grid=(n_rows // rows,),
        in_specs=[const] * 4,
        out_specs=[out] * 4,
        compiler_params=_cparams(("arbitrary",)),
        name="rope_tables",
    )(inv64, sg64, inv32, sg32)


def _proj_kernel(x_ref, win_ref, qn_ref, wqb_ref, wk_ref, kvn_ref, c64_ref, s64_ref, c32_ref, s32_ref,
                 qcat_ref, kcat_ref, lat_ref, rope_ref, qm_ref, km_ref, kaug_ref, vm_ref, vmb_ref, kmean_ref,
                 *, blocks_per_seq):
    tm = x_ref.shape[0]
    nsub = tm // LANE
    c64, s64, c32, s32 = c64_ref[...], s64_ref[...], c32_ref[...], s32_ref[...]
    h = _bdot(x_ref[...], win_ref[...])
    o_kv = Q_RANK
    o_qm = o_kv + KV_RANK
    o_km = o_qm + MOBA_WIDTH
    o_vm = o_km + MOBA_WIDTH
    o_pe = o_vm + MOBA_WIDTH

    qn = _rms_norm(h[:, :Q_RANK], qn_ref[...])
    q = _bdot(qn, wqb_ref[...])
    o_qr = MLA_HEADS * LANE
    qr = _rope(q[:, o_qr:o_qr + MLA_HEADS * ROPE_DIM], c32, s32, ROPE_DIM // 2)
    lane = lax.broadcasted_iota(jnp.int32, (tm, LANE), 1)
    heads_per_chunk = LANE // ROPE_DIM
    for hd in range(MLA_HEADS):
        ql = _bdot(q[:, hd * LANE:(hd + 1) * LANE], wk_ref[hd]).astype(BF16)
        grp = hd % heads_per_chunk
        chunk = qr[:, (hd // heads_per_chunk) * LANE:(hd // heads_per_chunk + 1) * LANE]
        qrh = jnp.where((lane >= grp * ROPE_DIM) & (lane < (grp + 1) * ROPE_DIM), chunk, 0.0).astype(BF16)
        for sb in range(nsub):
            qcat_ref[sb, hd, :, 0:LANE] = ql[sb * LANE:(sb + 1) * LANE]
            qcat_ref[sb, hd, :, LANE:2 * LANE] = qrh[sb * LANE:(sb + 1) * LANE]

    lat = _rms_norm(h[:, o_kv:o_kv + KV_RANK], kvn_ref[...])
    lat_ref[...] = lat
    kpe = _rope(h[:, o_pe:o_pe + LANE], c32, s32, ROPE_DIM // 2)
    rope_ref[...] = kpe[:, :ROPE_DIM]
    kcat_ref[:, 0:KV_RANK] = lat.astype(BF16)
    kcat_ref[:, KV_RANK:KV_RANK + LANE] = kpe.astype(BF16)

    qm_ref[...] = _rope(h[:, o_qm:o_qm + MOBA_WIDTH], c64, s64, MOBA_DIM // 2)
    km = _rope(h[:, o_km:o_km + MOBA_WIDTH], c64, s64, MOBA_DIM // 2)
    km_ref[...] = km
    kmean_ref[0] = jnp.mean(km, axis=0, keepdims=True)
    vm = h[:, o_vm:o_vm + MOBA_WIDTH]
    vm_ref[...] = vm
    vmb_ref[...] = vm.astype(BF16)
    blk = pl.program_id(0) % blocks_per_seq
    onehot = jnp.where(lane == MOBA_DIM + blk, 1.0, 0.0)
    for g in range(MOBA_HEADS // 2):
        ch = km[:, g * LANE:(g + 1) * LANE]
        kaug_ref[:, (2 * g) * LANE:(2 * g + 1) * LANE] = jnp.where(lane < MOBA_DIM, ch, onehot).astype(BF16)
        chs = pltpu.roll(ch, MOBA_DIM, axis=1)
        kaug_ref[:, (2 * g + 1) * LANE:(2 * g + 2) * LANE] = jnp.where(lane < MOBA_DIM, chs, onehot).astype(BF16)


def _attn_proj(x, w, tabs, seq_rows):
    n = x.shape[0]
    tm = ROW_TILE
    assert tm == MOBA_BLOCK and n % tm == 0 and seq_rows % tm == 0
    tiles_per_seq = seq_rows // tm
    nsub = tm // LANE
    c64, s64, c32, s32 = tabs
    row = lambda width: pl.BlockSpec((tm, width), lambda i: (i, 0))
    tab = pl.BlockSpec((tm, LANE), lambda i: (i % tiles_per_seq, 0))
    full = lambda a: pl.BlockSpec(a.shape, lambda i: (0,) * a.ndim)
    outs = pl.pallas_call(
        functools.partial(_proj_kernel, blocks_per_seq=tiles_per_seq),
        out_shape=[
            jax.ShapeDtypeStruct((n // LANE, MLA_HEADS, LANE, 2 * LANE), BF16),
            jax.ShapeDtypeStruct((n, 2 * LANE), BF16),
            jax.ShapeDtypeStruct((n, KV_RANK), F32),
            jax.ShapeDtypeStruct((n, ROPE_DIM), F32),
            jax.ShapeDtypeStruct((n, MOBA_WIDTH), F32),
            jax.ShapeDtypeStruct((n, MOBA_WIDTH), F32),
            jax.ShapeDtypeStruct((n, MOBA_HEADS * LANE), BF16),
            jax.ShapeDtypeStruct((n, MOBA_WIDTH), F32),
            jax.ShapeDtypeStruct((n, MOBA_WIDTH), BF16),
            jax.ShapeDtypeStruct((n // tm, 1, MOBA_WIDTH), F32),
        ],
        grid=(n // tm,),
        in_specs=[row(D_MODEL), full(w["w_in"]), full(w["q_norm"]), full(w["w_q_b"]), full(w["wk"]),
                  full(w["kv_norm"]), tab, tab, tab, tab],
        out_specs=[
            pl.BlockSpec((nsub, MLA_HEADS, LANE, 2 * LANE), lambda i: (i, 0, 0, 0)),
            row(2 * LANE), row(KV_RANK), row(ROPE_DIM), row(MOBA_WIDTH), row(MOBA_WIDTH),
            row(MOBA_HEADS * LANE), row(MOBA_WIDTH), row(MOBA_WIDTH),
            pl.BlockSpec((1, 1, MOBA_WIDTH), lambda i: (i, 0, 0)),
        ],
        compiler_params=_cparams(("arbitrary",)),
        name="attn_proj",
    )(x, w["w_in"], w["q_norm"], w["w_q_b"], w["wk"], w["kv_norm"], c64, s64, c32, s32)
    return outs


def _mla_prompt_kernel(q_ref, kcat_ref, o_ref, m_ref, l_ref, acc_ref):
    qi = pl.program_id(1)
    rows = MLA_HEADS * MLA_TQ
    q = q_ref[0].reshape(rows, 2 * LANE)
    cexp = MLA_SCALE * LOG2E

    def scores(kt):
        k = kcat_ref[pl.ds(pl.multiple_of(kt * MLA_TK, MLA_TK), MLA_TK), :]
        return _bdot_nt(q, k), k[:, :KV_RANK]

    kd = (qi * MLA_TQ) // MLA_TK
    s, v = scores(kd)
    col = kd * MLA_TK + lax.broadcasted_iota(jnp.int32, (rows, MLA_TK), 1)
    rowpos = qi * MLA_TQ + lax.broadcasted_iota(jnp.int32, (rows, MLA_TK), 0) % MLA_TQ
    s = jnp.where(col <= rowpos, s, NEG)
    m = jnp.max(s, axis=1, keepdims=True)
    p = jnp.exp2((s - m) * cexp)
    m_ref[...] = m
    l_ref[...] = jnp.sum(p, axis=1, keepdims=True)
    acc_ref[...] = jnp.dot(p.astype(BF16), v, preferred_element_type=F32)

    def body(kt, carry):
        s, v = scores(kt)
        m_old = m_ref[...]
        m_new = jnp.maximum(m_old, jnp.max(s, axis=1, keepdims=True))
        alpha = jnp.exp2((m_old - m_new) * cexp)
        p = jnp.exp2((s - m_new) * cexp)
        m_ref[...] = m_new
        l_ref[...] = l_ref[...] * alpha + jnp.sum(p, axis=1, keepdims=True)
        acc_ref[...] = acc_ref[...] * alpha + jnp.dot(p.astype(BF16), v, preferred_element_type=F32)
        return carry

    lax.fori_loop(0, kd, body, 0)
    o = (acc_ref[...] / l_ref[...]).astype(BF16)
    o_ref[0] = o.reshape(MLA_HEADS, MLA_TQ, KV_RANK)


def _mla_prompt(qcat, kcat, batch, seq):
    assert MLA_TQ == LANE and seq % MLA_TK == 0 and MLA_TK % MLA_TQ == 0
    nq = seq // MLA_TQ
    rows = MLA_HEADS * MLA_TQ
    return pl.pallas_call(
        _mla_prompt_kernel,
        out_shape=jax.ShapeDtypeStruct((batch * nq, MLA_HEADS, MLA_TQ, KV_RANK), BF16),
        grid=(batch, nq),
        in_specs=[pl.BlockSpec((1, MLA_HEADS, MLA_TQ, 2 * LANE), lambda b, i: (b * nq + i, 0, 0, 0)),
                  pl.BlockSpec((seq, 2 * LANE), lambda b, i: (b, 0))],
        out_specs=pl.BlockSpec((1, MLA_HEADS, MLA_TQ, KV_RANK), lambda b, i: (b * nq + i, 0, 0, 0)),
        scratch_shapes=[pltpu.VMEM((rows, 1), F32), pltpu.VMEM((rows, 1), F32), pltpu.VMEM((rows, KV_RANK), F32)],
        compiler_params=_cparams(("arbitrary", "arbitrary")),
        name="mla_prompt",
    )(qcat, kcat)


def _class_reduce(v, op):
    for sh in (8, 16, 32, 64):
        v = op(v, pltpu.roll(v, sh, axis=1))
    return v


def _moba_select_kernel(qm_ref, kmbd_ref, perm_ref, qaug_ref, *, blocks_per_seq):
    tm = qm_ref.shape[0]
    qm = qm_ref[...]
    gate = jnp.dot(qm, kmbd_ref[...], precision=lax.Precision.HIGHEST, preferred_element_type=F32)
    lane = lax.broadcasted_iota(jnp.int32, (tm, LANE), 1)
    nblk = lane // MOBA_HEADS
    own = pl.program_id(0) % blocks_per_seq
    past = nblk < own
    g = jnp.where(past, gate, -jnp.inf)
    picked = jnp.zeros((tm, LANE), F32)
    for _ in range(MOBA_TOPK):
        top = _class_reduce(g, jnp.maximum)
        first = _class_reduce(jnp.where(g == top, nblk, LANE), jnp.minimum)
        hit = nblk == first
        picked = picked + jnp.where(hit & (top > -jnp.inf), 1.0, 0.0)
        g = jnp.where(hit, -jnp.inf, g)
    masked = jnp.where(past & (picked < 0.5), 1.0, 0.0)
    bias = jnp.dot(masked.astype(BF16), perm_ref[...], preferred_element_type=F32) * NEG
    for gp in range(MOBA_HEADS // 2):
        ch = qm[:, gp * LANE:(gp + 1) * LANE]
        for e, src in ((0, ch), (1, pltpu.roll(ch, MOBA_DIM, axis=1))):
            hd = 2 * gp + e
            slot = jnp.where(lane < MOBA_DIM, src, 0.0) + bias[:, hd * LANE:(hd + 1) * LANE]
            qaug_ref[:, hd * LANE:(hd + 1) * LANE] = slot.astype(BF16)


def _moba_select(qm, kmean, batch, seq):
    n = qm.shape[0]
    tm = ROW_TILE
    nblk = seq // MOBA_BLOCK
    assert tm == MOBA_BLOCK and nblk * MOBA_HEADS <= LANE
    km4 = kmean.reshape(batch, nblk, MOBA_HEADS, MOBA_DIM)
    eye = jnp.eye(MOBA_HEADS, dtype=F32)
    kmbd = jnp.einsum("bnhd,hk->bhdnk", km4, eye).reshape(batch, MOBA_WIDTH, nblk * MOBA_HEADS)
    kmbd = jnp.pad(kmbd, ((0, 0), (0, 0), (0, LANE - nblk * MOBA_HEADS)))
    src = jnp.arange(LANE)
    dst = (src % MOBA_HEADS) * LANE + MOBA_DIM + src // MOBA_HEADS
    perm = (jnp.arange(MOBA_HEADS * LANE)[None, :] == dst[:, None]).astype(BF16)
    return pl.pallas_call(
        functools.partial(_moba_select_kernel, blocks_per_seq=nblk),
        out_shape=jax.ShapeDtypeStruct((n, MOBA_HEADS * LANE), BF16),
        grid=(n // tm,),
        in_specs=[pl.BlockSpec((tm, MOBA_WIDTH), lambda i: (i, 0)),
                  pl.BlockSpec((None, MOBA_WIDTH, LANE), lambda i: (i // nblk, 0, 0)),
                  pl.BlockSpec(perm.shape, lambda i: (0, 0))],
        out_specs=pl.BlockSpec((tm, MOBA_HEADS * LANE), lambda i: (i, 0)),
        compiler_params=_cparams(("arbitrary",)),
        name="moba_select",
    )(qm, kmbd, perm)


def _moba_prompt_kernel(qaug_ref, kaug_ref, v_ref, o_ref, m_ref, l_ref, acc_ref):
    own = pl.program_id(1)
    tq = qaug_ref.shape[0]
    cexp = MOBA_SCALE * LOG2E
    lane = lax.broadcasted_iota(jnp.int32, (tq, LANE), 1)

    def scores(n, hd):
        r0 = pl.multiple_of(n * MOBA_BLOCK, MOBA_BLOCK)
        q = qaug_ref[:, hd * LANE:(hd + 1) * LANE]
        k = kaug_ref[pl.ds(r0, MOBA_BLOCK), hd * LANE:(hd + 1) * LANE]
        v = v_ref[pl.ds(r0, MOBA_BLOCK), (hd // 2) * LANE:(hd // 2 + 1) * LANE]
        return _bdot_nt(q, k), v

    tri = (lax.broadcasted_iota(jnp.int32, (tq, MOBA_BLOCK), 1)
           <= lax.broadcasted_iota(jnp.int32, (tq, MOBA_BLOCK), 0))
    for hd in range(MOBA_HEADS):
        s, v = scores(own, hd)
        s = jnp.where(tri, s, NEG)
        m = jnp.max(s, axis=1, keepdims=True)
        p = jnp.exp2((s - m) * cexp)
        m_ref[hd] = m
        l_ref[hd] = jnp.sum(p, axis=1, keepdims=True)
        acc_ref[hd] = jnp.dot(p.astype(BF16), v, preferred_element_type=F32)

    def body(n, carry):
        for hd in range(MOBA_HEADS):
            s, v = scores(n, hd)
            m_old = m_ref[hd]
            m_new = jnp.maximum(m_old, jnp.max(s, axis=1, keepdims=True))
            alpha = jnp.exp2((m_old - m_new) * cexp)
            p = jnp.exp2((s - m_new) * cexp)
            m_ref[hd] = m_new
            l_ref[hd] = l_ref[hd] * alpha + jnp.sum(p, axis=1, keepdims=True)
            acc_ref[hd] = acc_ref[hd] * alpha + jnp.dot(p.astype(BF16), v, preferred_element_type=F32)
        return carry

    lax.fori_loop(0, own, body, 0)
    for gp in range(MOBA_HEADS // 2):
        lo = acc_ref[2 * gp] / l_ref[2 * gp]
        hi = acc_ref[2 * gp + 1] / l_ref[2 * gp + 1]
        o_ref[:, gp * LANE:(gp + 1) * LANE] = jnp.where(lane < MOBA_DIM, lo, hi).astype(BF16)


def _moba_prompt(qaug, kaug, vmb, batch, seq):
    tq = MOBA_BLOCK
    nq = seq // tq
    return pl.pallas_call(
        _moba_prompt_kernel,
        out_shape=jax.ShapeDtypeStruct((batch * seq, MOBA_WIDTH), BF16),
        grid=(batch, nq),
        in_specs=[pl.BlockSpec((tq, MOBA_HEADS * LANE), lambda b, i: (b * nq + i, 0)),
                  pl.BlockSpec((seq, MOBA_HEADS * LANE), lambda b, i: (b, 0)),
                  pl.BlockSpec((seq, MOBA_WIDTH), lambda b, i: (b, 0))],
        out_specs=pl.BlockSpec((tq, MOBA_WIDTH), lambda b, i: (b * nq + i, 0)),
        scratch_shapes=[pltpu.VMEM((MOBA_HEADS, tq, 1), F32), pltpu.VMEM((MOBA_HEADS, tq, 1), F32),
                        pltpu.VMEM((MOBA_HEADS, tq, LANE), F32)],
        compiler_params=_cparams(("arbitrary", "arbitrary")),
        name="moba_prompt",
    )(qaug, kaug, vmb)


def _attn_out_kernel(x_ref, omla_ref, omoba_ref, wv_ref, wout_ref, g_ref, b_ref, y_ref):
    tm = x_ref.shape[0]
    x = x_ref[...]
    mix = _bdot(omoba_ref[...], wout_ref[MLA_HEADS * V_DIM:, :])
    for gp in range(MLA_HEADS // 2):
        v2 = jnp.zeros((tm, LANE), F32)
        for e in range(2):
            hd = 2 * gp + e
            o = omla_ref[:, hd].reshape(tm, KV_RANK)
            v2 = v2 + jnp.dot(o, wv_ref[hd], preferred_element_type=F32)
        mix = mix + _bdot(v2, wout_ref[gp * LANE:(gp + 1) * LANE, :])
    y_ref[...] = _layer_norm(ALPHA * x + mix, g_ref[...], b_ref[...])


def _attn_out(x, omla, omoba, w, g, b):
    n = x.shape[0]
    tm = ROW_TILE
    nsub = tm // LANE
    full = lambda a: pl.BlockSpec(a.shape, lambda i: (0,) * a.ndim)
    return pl.pallas_call(
        _attn_out_kernel,
        out_shape=jax.ShapeDtypeStruct((n, D_MODEL), F32),
        grid=(n // tm,),
        in_specs=[pl.BlockSpec((tm, D_MODEL), lambda i: (i, 0)),
                  pl.BlockSpec((nsub, MLA_HEADS, LANE, KV_RANK), lambda i: (i, 0, 0, 0)),
                  pl.BlockSpec((tm, MOBA_WIDTH), lambda i: (i, 0)),
                  full(w["wv"]), full(w["w_out"]), full(g), full(b)],
        out_specs=pl.BlockSpec((tm, D_MODEL), lambda i: (i, 0)),
        compiler_params=_cparams(("arbitrary",)),
        name="attn_out",
    )(x, omla, omoba, w["wv"], w["w_out"], g, b)


def _mla_sample_kernel(pt_ref, qlat_ref, qrope_ref, latn_ref, ropen_ref, *rest, n_new):
    pps = PAGES_PER_STEP
    lat_refs, rope_refs = rest[:pps], rest[pps:2 * pps]
    o_ref, m_ref, l_ref, acc_ref = rest[2 * pps:]
    j = pl.program_id(1)
    cexp = MLA_SCALE * LOG2E
    ql = qlat_ref[0]
    qr = qrope_ref[0]
    rows = ql.shape[0]

    @pl.when(j == 0)
    def _():
        m_ref[...] = jnp.full((rows, 1), NEG, F32)
        l_ref[...] = jnp.zeros((rows, 1), F32)
        acc_ref[...] = jnp.zeros((rows, KV_RANK), F32)

    lat = jnp.concatenate([r[...] for r in lat_refs], axis=0).astype(BF16)
    rpt = jnp.concatenate([r[...] for r in rope_refs], axis=1)
    s = _bdot_nt(ql, lat) + _bdot(qr, rpt)
    m_old = m_ref[...]
    m_new = jnp.maximum(m_old, jnp.max(s, axis=1, keepdims=True))
    alpha = jnp.exp2((m_old - m_new) * cexp)
    p = jnp.exp2((s - m_new) * cexp)
    m_ref[...] = m_new
    l_ref[...] = l_ref[...] * alpha + jnp.sum(p, axis=1, keepdims=True)
    acc_ref[...] = acc_ref[...] * alpha + jnp.dot(p.astype(BF16), lat, preferred_element_type=F32)

    @pl.when(j == pl.num_programs(1) - 1)
    def _():
        latn = latn_ref[0].astype(BF16)
        s = _bdot_nt(ql, latn) + _bdot_nt(qr, ropen_ref[0])
        tq = lax.broadcasted_iota(jnp.int32, s.shape, 0) // MLA_HEADS
        tk = lax.broadcasted_iota(jnp.int32, s.shape, 1)
        s = jnp.where(tk <= tq, s, NEG)
        m_old = m_ref[...]
        m_new = jnp.maximum(m_old, jnp.max(s, axis=1, keepdims=True))
        alpha = jnp.exp2((m_old - m_new) * cexp)
        p = jnp.exp2((s - m_new) * cexp)
        l = l_ref[...] * alpha + jnp.sum(p, axis=1, keepdims=True)
        acc = acc_ref[...] * alpha + jnp.dot(p.astype(BF16), latn, preferred_element_type=F32)
        o_ref[0] = (acc / l).astype(BF16)


def _mla_sample(page_table, qlat, qrope, lat_new, rope_new, pool_lat, pool_rope_t, layer):
    nb, rows, _ = qlat.shape
    n_new = lat_new.shape[1]
    n_pages = page_table.shape[1]
    pps = PAGES_PER_STEP
    assert n_pages % pps == 0
    pt = page_table.reshape(-1)

    def page(page_rows, i):
        return pl.BlockSpec((None, None, page_rows, PAGE_SIZE),
                            lambda b, j, pt_ref: (layer, pt_ref[b * n_pages + j * pps + i], 0, 0))

    per_seq = lambda a: pl.BlockSpec((1,) + a.shape[1:], lambda b, j, pt_ref: (b, 0, 0))
    grid_spec = pltpu.PrefetchScalarGridSpec(
        num_scalar_prefetch=1,
        grid=(nb, n_pages // pps),
        in_specs=[per_seq(qlat), per_seq(qrope), per_seq(lat_new), per_seq(rope_new)]
                 + [page(PAGE_SIZE, i) for i in range(pps)] + [page(ROPE_DIM, i) for i in range(pps)],
        out_specs=pl.BlockSpec((1, rows, KV_RANK), lambda b, j, pt_ref: (b, 0, 0)),
        scratch_shapes=[pltpu.VMEM((rows, 1), F32), pltpu.VMEM((rows, 1), F32), pltpu.VMEM((rows, KV_RANK), F32)],
    )
    return pl.pallas_call(
        functools.partial(_mla_sample_kernel, n_new=n_new),
        out_shape=jax.ShapeDtypeStruct((nb, rows, KV_RANK), BF16),
        grid_spec=grid_spec,
        compiler_params=_cparams(("arbitrary", "arbitrary")),
        name="mla_sample",
    )(pt, qlat, qrope, lat_new, rope_new, *([pool_lat] * pps), *([pool_rope_t] * pps))


def _moba_sample_kernel(pt_ref, qbd_ref, qbdf_ref, kn_ref, vn_ref, expand_ref, *rest, n_new, n_past_blocks):
    pps = PAGES_PER_STEP
    k_refs, v_refs = rest[:pps], rest[pps:2 * pps]
    o_ref, s_ref, ksum_ref, p_ref, pn_ref, den_ref, acc_ref = rest[2 * pps:]
    phase = pl.program_id(1)
    j = pl.program_id(2)
    cexp = MOBA_SCALE * LOG2E
    qbd = qbd_ref[0]
    pages_per_block = MOBA_BLOCK // PAGE_SIZE

    @pl.when((phase == 0) & (j == 0))
    def _():
        ksum_ref[...] = jnp.zeros(ksum_ref.shape, F32)

    @pl.when(phase == 0)
    def _():
        lane = lax.broadcasted_iota(jnp.int32, ksum_ref.shape, 1)
        for i0 in range(0, pps, pages_per_block):
            ksum = None
            for i in range(i0, i0 + pages_per_block):
                kt = k_refs[i][...]
                col = pl.multiple_of((j * pps + i) * PAGE_SIZE, PAGE_SIZE)
                s_ref[:, pl.ds(col, PAGE_SIZE)] = _bdot(qbd, kt)
                ksum = kt if ksum is None else ksum + kt
            blk = (j * pps + i0) // pages_per_block
            ksum_ref[...] = jnp.where(lane == blk, jnp.sum(ksum, axis=1, keepdims=True), ksum_ref[...])

    @pl.when((phase == 1) & (j == 0))
    def _():
        kmean = ksum_ref[...] * (1.0 / MOBA_BLOCK)
        gate = jnp.dot(qbdf_ref[0], kmean, precision=lax.Precision.HIGHEST, preferred_element_type=F32)
        nidx = lax.broadcasted_iota(jnp.int32, gate.shape, 1)
        g = jnp.where(nidx < n_past_blocks, gate, -jnp.inf)
        nidx = nidx.astype(F32)
        picked = jnp.zeros(gate.shape, F32)
        for _ in range(MOBA_TOPK):
            top = jnp.max(g, axis=1, keepdims=True)
            first = jnp.min(jnp.where(g == top, nidx, float(LANE)), axis=1, keepdims=True)
            hit = nidx == first
            picked = picked + jnp.where(hit & (top > -jnp.inf), 1.0, 0.0)
            g = jnp.where(hit, -jnp.inf, g)
        chosen = jnp.dot(picked.astype(BF16), expand_ref[...], preferred_element_type=F32)
        s = jnp.where(chosen > 0.5, s_ref[...], NEG)
        sn = _bdot_nt(qbd, kn_ref[0])
        tq = lax.broadcasted_iota(jnp.int32, sn.shape, 0) // MOBA_HEADS
        tk = lax.broadcasted_iota(jnp.int32, sn.shape, 1)
        sn = jnp.where(tk <= tq, sn, NEG)
        m = jnp.maximum(jnp.max(s, axis=1, keepdims=True), jnp.max(sn, axis=1, keepdims=True))
        p = jnp.exp2((s - m) * cexp)
        pn = jnp.exp2((sn - m) * cexp)
        den_ref[...] = jnp.sum(p, axis=1, keepdims=True) + jnp.sum(pn, axis=1, keepdims=True)
        p_ref[...] = p.astype(BF16)
        pn_ref[...] = pn
        acc_ref[...] = jnp.zeros(acc_ref.shape, F32)

    @pl.when(phase == 1)
    def _():
        acc = acc_ref[...]
        for i in range(pps):
            col = pl.multiple_of((j * pps + i) * PAGE_SIZE, PAGE_SIZE)
            acc = acc + _bdot_nt(p_ref[:, pl.ds(col, PAGE_SIZE)], v_refs[i][...])
        acc_ref[...] = acc

    @pl.when((phase == 1) & (j == pl.num_programs(2) - 1))
    def _():
        o = (acc_ref[...] + _bdot(pn_ref[...], vn_ref[0])) / den_ref[...]
        hrow = lax.broadcasted_iota(jnp.int32, o.shape, 0) % MOBA_HEADS
        hcol = lax.broadcasted_iota(jnp.int32, o.shape, 1) // MOBA_DIM
        o = jnp.where(hrow == hcol, o, 0.0)
        for t in range(n_new):
            o_ref[0, t:t + 1, :] = jnp.sum(o[t * MOBA_HEADS:(t + 1) * MOBA_HEADS], axis=0,
                                           keepdims=True).astype(BF16)


def _moba_sample(page_table, qbd, qbdf, k_new, v_new, pool_kt, pool_vt, layer):
    nb, rows, _ = qbd.shape
    n_new = k_new.shape[1]
    n_pages = page_table.shape[1]
    pps = PAGES_PER_STEP
    past_len = n_pages * PAGE_SIZE
    pages_per_block = MOBA_BLOCK // PAGE_SIZE
    assert n_pages % pps == 0 and pps % pages_per_block == 0 and past_len % MOBA_BLOCK == 0 and n_new <= MOBA_BLOCK
    n_past_blocks = past_len // MOBA_BLOCK
    assert n_past_blocks <= LANE
    steps = n_pages // pps
    pt = page_table.reshape(-1)
    expand = (jnp.arange(past_len)[None, :] // MOBA_BLOCK == jnp.arange(LANE)[:, None]).astype(BF16)

    def kpage(i):
        return pl.BlockSpec((None, None, MOBA_WIDTH, PAGE_SIZE),
                            lambda b, ph, j, pt_ref: (layer, pt_ref[b * n_pages + (j * (1 - ph) + (steps - 1) * ph) * pps + i], 0, 0))

    def vpage(i):
        return pl.BlockSpec((None, None, MOBA_WIDTH, PAGE_SIZE),
                            lambda b, ph, j, pt_ref: (layer, pt_ref[b * n_pages + j * ph * pps + i], 0, 0))

    per_seq = lambda a: pl.BlockSpec((1,) + a.shape[1:], lambda b, ph, j, pt_ref: (b, 0, 0))
    grid_spec = pltpu.PrefetchScalarGridSpec(
        num_scalar_prefetch=1,
        grid=(nb, 2, steps),
        in_specs=[per_seq(qbd), per_seq(qbdf), per_seq(k_new), per_seq(v_new),
                  pl.BlockSpec(expand.shape, lambda b, ph, j, pt_ref: (0, 0))]
                 + [kpage(i) for i in range(pps)] + [vpage(i) for i in range(pps)],
        out_specs=pl.BlockSpec((1, n_new, MOBA_WIDTH), lambda b, ph, j, pt_ref: (b, 0, 0)),
        scratch_shapes=[pltpu.VMEM((rows, past_len), F32),
                        pltpu.VMEM((MOBA_WIDTH, LANE), F32),
                        pltpu.VMEM((rows, past_len), BF16),
                        pltpu.VMEM((rows, n_new), F32),
                        pltpu.VMEM((rows, 1), F32),
                        pltpu.VMEM((rows, MOBA_WIDTH), F32)],
    )
    return pl.pallas_call(
        functools.partial(_moba_sample_kernel, n_new=n_new, n_past_blocks=n_past_blocks),
        out_shape=jax.ShapeDtypeStruct((nb, n_new, MOBA_WIDTH), BF16),
        grid_spec=grid_spec,
        compiler_params=_cparams(("arbitrary", "arbitrary", "arbitrary")),
        name="moba_sample",
    )(pt, qbd, qbdf, k_new, v_new, expand, *([pool_kt] * pps), *([pool_vt] * pps))


def _s5_param_kernel(are_c_ref, aim_c_ref, dt_c_ref, bre_ref, bim_ref, are_r_ref, aim_r_ref, dt_r_ref,
                     bbre_ref, bbim_ref, pw_ref, step_ref):
    def abar(are, aim, dt):
        mag = jnp.exp(are * dt)
        return mag * jnp.cos(aim * dt), mag * jnp.sin(aim * dt)

    are, aim = are_c_ref[...], aim_c_ref[...]
    ar, ai = abar(are, aim, dt_c_ref[...])
    nr, ni = ar - 1.0, ai
    den = are * are + aim * aim
    cr = (nr * are + ni * aim) / den
    ci = (ni * are - nr * aim) / den
    bre, bim = bre_ref[...], bim_ref[...]
    bbre_ref[...] = cr * bre - ci * bim
    bbim_ref[...] = cr * bim + ci * bre
    ar, ai = abar(are_r_ref[...], aim_r_ref[...], dt_r_ref[...])
    pr, pi = ar, ai
    powers = []
    for i in range(SUBLANE):
        pw_ref[0, i:i + 1, :] = pr
        pw_ref[1, i:i + 1, :] = pi
        powers.append((pr, pi))
        pr, pi = pr * ar - pi * ai, pr * ai + pi * ar
    zero = jnp.zeros_like(ar)
    for k in range(3):
        d = 1 << k
        for i in range(SUBLANE):
            step_ref[2 * k, i:i + 1, :] = powers[d - 1][0] if i >= d else zero
            step_ref[2 * k + 1, i:i + 1, :] = powers[d - 1][1] if i >= d else zero


def _s5_params(a_re, a_im, log_dt, b_re, b_im, c_re, c_im):
    col = lambda a: a.reshape(SSM_CH, 1)
    rowv = lambda a: a.reshape(1, SSM_CH)
    dt = jnp.exp(log_dt.astype(F32))
    dt_full = jnp.broadcast_to(dt[:, None], (SSM_GROUPS, SSM_STATE))
    vm = lambda shape: pl.BlockSpec(shape, lambda: (0,) * len(shape))
    ins = [col(a_re), col(a_im), col(dt_full), b_re.reshape(SSM_CH, SSM_GROUP), b_im.reshape(SSM_CH, SSM_GROUP),
           rowv(a_re), rowv(a_im), rowv(dt_full)]
    bbre, bbim, pw, step = pl.pallas_call(
        _s5_param_kernel,
        out_shape=[jax.ShapeDtypeStruct((SSM_CH, SSM_GROUP), F32), jax.ShapeDtypeStruct((SSM_CH, SSM_GROUP), F32),
                   jax.ShapeDtypeStruct((2, SUBLANE, SSM_CH), F32), jax.ShapeDtypeStruct((6, SUBLANE, SSM_CH), F32)],
        in_specs=[vm(a.shape) for a in ins],
        out_specs=[vm((SSM_CH, SSM_GROUP)), vm((SSM_CH, SSM_GROUP)), vm((2, SUBLANE, SSM_CH)), vm((6, SUBLANE, SSM_CH))],
        compiler_params=pltpu.CompilerParams(vmem_limit_bytes=VMEM_LIMIT),
        name="s5_params",
    )(*ins)
    gpc = SSM_GROUPS // S5_CHUNKS
    eye = jnp.eye(gpc, dtype=F32)

    def in_mat(bb):
        b4 = bb.reshape(S5_CHUNKS, gpc, SSM_STATE, SSM_GROUP)
        return jnp.einsum("kgpc,gh->kgchp", b4, eye).reshape(S5_CHUNKS, S5_CIN, S5_CST)

    def out_mat(cc):
        c4 = cc.reshape(S5_CHUNKS, gpc, SSM_GROUP, SSM_STATE)
        return jnp.einsum("kgcp,gh->kgphc", c4, eye).reshape(S5_CHUNKS, S5_CST, S5_CIN)

    wb = jnp.concatenate([in_mat(bbre), in_mat(bbim)], axis=-1).astype(BF16)
    return dict(wb=wb, wc_re=out_mat(c_re.astype(F32)).astype(BF16), wc_im=out_mat(c_im.astype(F32)).astype(BF16),
                pw=pw, step=step)


def _s5_tail(x, y, d_ref, wglu_ref, g_ref, b_ref):
    z = _gelu(y + d_ref[...] * x)
    zz = _bdot(z, wglu_ref[...])
    out = zz[:, :D_MODEL] * _sigmoid(zz[:, D_MODEL:])
    return _layer_norm(ALPHA * x + out, g_ref[...], b_ref[...])


def _s5_prompt_kernel(x_ref, wb_ref, wcre_ref, wcim_ref, pw_ref, step_ref, d_ref, wglu_ref, g_ref, b_ref,
                      y_ref, hl_ref, carry_ref, h_scr):
    tt = x_ref.shape[0]

    @pl.when(pl.program_id(1) == 0)
    def _():
        carry_ref[...] = jnp.zeros(carry_ref.shape, F32)

    x = x_ref[...]
    ys = []
    for k in range(S5_CHUNKS):
        lo = k * S5_CST
        h_scr[...] = _bdot(x[:, k * S5_CIN:(k + 1) * S5_CIN], wb_ref[k])
        pr, pi = pw_ref[0, :, lo:lo + S5_CST], pw_ref[1, :, lo:lo + S5_CST]
        steps = [(step_ref[2 * s, :, lo:lo + S5_CST], step_ref[2 * s + 1, :, lo:lo + S5_CST], 1 << s)
                 for s in range(3)]

        def block(i, carry, pr=pr, pi=pi, steps=steps):
            cr, ci = carry
            r0 = pl.multiple_of(i * SUBLANE, SUBLANE)
            hr = h_scr[pl.ds(r0, SUBLANE), 0:S5_CST]
            hi = h_scr[pl.ds(r0, SUBLANE), S5_CST:2 * S5_CST]
            for sr, si, sh in steps:
                rr, ri = pltpu.roll(hr, sh, axis=0), pltpu.roll(hi, sh, axis=0)
                hr, hi = hr + sr * rr - si * ri, hi + sr * ri + si * rr
            hr, hi = hr + pr * cr - pi * ci, hi + pr * ci + pi * cr
            h_scr[pl.ds(r0, SUBLANE), 0:S5_CST] = hr
            h_scr[pl.ds(r0, SUBLANE), S5_CST:2 * S5_CST] = hi
            return hr[SUBLANE - 1:SUBLANE], hi[SUBLANE - 1:SUBLANE]

        cr, ci = lax.fori_loop(0, tt // SUBLANE, block,
                               (carry_ref[k, 0:1, 0:S5_CST], carry_ref[k, 0:1, S5_CST:2 * S5_CST]))
        carry_ref[k, 0:1, 0:S5_CST] = cr
        carry_ref[k, 0:1, S5_CST:2 * S5_CST] = ci
        ys.append(_bdot(h_scr[:, 0:S5_CST], wcre_ref[k]) - _bdot(h_scr[:, S5_CST:2 * S5_CST], wcim_ref[k]))
    y_ref[...] = _s5_tail(x, jnp.concatenate(ys, axis=-1), d_ref, wglu_ref, g_ref, b_ref)
    hl_ref[0] = carry_ref[...]


def _s5_prompt(x, sp, d, wglu, g, b, batch, seq):
    tt = ROW_TILE
    nt = seq // tt
    full = lambda a: pl.BlockSpec(a.shape, lambda bb, i: (0,) * a.ndim)
    y, hl = pl.pallas_call(
        _s5_prompt_kernel,
        out_shape=[jax.ShapeDtypeStruct((batch * seq, D_MODEL), F32),
                   jax.ShapeDtypeStruct((batch, S5_CHUNKS, SUBLANE, 2 * S5_CST), F32)],
        grid=(batch, nt),
        in_specs=[pl.BlockSpec((tt, D_MODEL), lambda bb, i: (bb * nt + i, 0)),
                  full(sp["wb"]), full(sp["wc_re"]), full(sp["wc_im"]), full(sp["pw"]), full(sp["step"]),
                  full(d), full(wglu), full(g), full(b)],
        out_specs=[pl.BlockSpec((tt, D_MODEL), lambda bb, i: (bb * nt + i, 0)),
                   pl.BlockSpec((1, S5_CHUNKS, SUBLANE, 2 * S5_CST), lambda bb, i: (bb, 0, 0, 0))],
        scratch_shapes=[pltpu.VMEM((S5_CHUNKS, SUBLANE, 2 * S5_CST), F32), pltpu.VMEM((tt, 2 * S5_CST), F32)],
        compiler_params=_cparams(("arbitrary", "arbitrary")),
        name="s5_prompt",
    )(x, sp["wb"], sp["wc_re"], sp["wc_im"], sp["pw"], sp["step"], d, wglu, g, b)
    h_re = hl[:, :, 0, :S5_CST].reshape(batch, SSM_GROUPS, SSM_STATE)
    h_im = hl[:, :, 0, S5_CST:].reshape(batch, SSM_GROUPS, SSM_STATE)
    return y, h_re, h_im


def _s5_sample_kernel(x_ref, h0re_ref, h0im_ref, wb_ref, wcre_ref, wcim_ref, pw_ref, d_ref, wglu_ref, g_ref, b_ref,
                      y_ref, hre_ref, him_ref):
    @pl.when(pl.program_id(0) == 0)
    def _():
        hre_ref[...] = h0re_ref[...]
        him_ref[...] = h0im_ref[...]

    x = x_ref[...]
    ys = []
    for k in range(S5_CHUNKS):
        lo = k * S5_CST
        bu = _bdot(x[:, k * S5_CIN:(k + 1) * S5_CIN], wb_ref[k])
        ar, ai = pw_ref[0, 0:1, lo:lo + S5_CST], pw_ref[1, 0:1, lo:lo + S5_CST]
        hr, hi = hre_ref[:, lo:lo + S5_CST], him_ref[:, lo:lo + S5_CST]
        nr = ar * hr - ai * hi + bu[:, :S5_CST]
        ni = ar * hi + ai * hr + bu[:, S5_CST:]
        hre_ref[:, lo:lo + S5_CST] = nr
        him_ref[:, lo:lo + S5_CST] = ni
        ys.append(_bdot(nr, wcre_ref[k]) - _bdot(ni, wcim_ref[k]))
    y_ref[...] = _s5_tail(x, jnp.concatenate(ys, axis=-1), d_ref, wglu_ref, g_ref, b_ref)


def _s5_sample(x, h0_re, h0_im, sp, d, wglu, g, b, nb, n_new):
    full = lambda a: pl.BlockSpec(a.shape, lambda t: (0,) * a.ndim)
    state = pl.BlockSpec((nb, SSM_CH), lambda t: (0, 0))
    y, h_re, h_im = pl.pallas_call(
        _s5_sample_kernel,
        out_shape=[jax.ShapeDtypeStruct((n_new * nb, D_MODEL), F32),
                   jax.ShapeDtypeStruct((nb, SSM_CH), F32), jax.ShapeDtypeStruct((nb, SSM_CH), F32)],
        grid=(n_new,),
        in_specs=[pl.BlockSpec((nb, D_MODEL), lambda t: (t, 0)), state, state,
                  full(sp["wb"]), full(sp["wc_re"]), full(sp["wc_im"]), full(sp["pw"]),
                  full(d), full(wglu), full(g), full(b)],
        out_specs=[pl.BlockSpec((nb, D_MODEL), lambda t: (t, 0)), state, state],
        compiler_params=_cparams(("arbitrary",)),
        name="s5_sample",
    )(x, h0_re.reshape(nb, SSM_CH), h0_im.reshape(nb, SSM_CH), sp["wb"], sp["wc_re"], sp["wc_im"], sp["pw"],
      d, wglu, g, b)
    return y, h_re.reshape(nb, SSM_GROUPS, SSM_STATE), h_im.reshape(nb, SSM_GROUPS, SSM_STATE)


def _ffn_prompt_kernel(x_ref, wup_ref, cw_ref, cb_ref, wdn_ref, g_ref, b_ref, y_ref, st_ref,
                       carry_ref, ua_ref, ub_ref, act_ref):
    tm = x_ref.shape[0]

    @pl.when(pl.program_id(1) == 0)
    def _():
        carry_ref[...] = jnp.zeros(carry_ref.shape, F32)

    x = x_ref[...]
    xb = x.astype(BF16)

    def conv(col0, scr):
        cols = slice(col0, col0 + FFN_CW)
        u = jnp.dot(xb, wup_ref[:, cols], preferred_element_type=F32)
        scr[0:SUBLANE, :] = carry_ref[:, cols]
        scr[SUBLANE:SUBLANE + tm, :] = u
        carry_ref[:, cols] = u[tm - SUBLANE:tm]
        p1 = scr[SUBLANE - 1:SUBLANE - 1 + tm, :]
        p2 = scr[SUBLANE - 2:SUBLANE - 2 + tm, :]
        return cb_ref[:, cols] + cw_ref[0:1, cols] * p2 + cw_ref[1:2, cols] * p1 + cw_ref[2:3, cols] * u

    for c in range(D_FF // FFN_CW):
        act = _gelu(conv(c * FFN_CW, ua_ref)) * conv(D_FF + c * FFN_CW, ub_ref)
        act_ref[:, c * FFN_CW:(c + 1) * FFN_CW] = act.astype(BF16)
    down = jnp.dot(act_ref[...], wdn_ref[...], preferred_element_type=F32)
    y_ref[...] = _layer_norm(ALPHA * x + down, g_ref[...], b_ref[...])
    st_ref[0] = carry_ref[...]


def _ffn_prompt(x, w, g, b, batch, seq):
    tm = ROW_TILE
    nt = seq // tm
    assert CONV_W - 1 <= SUBLANE and D_FF % FFN_CW == 0
    full = lambda a: pl.BlockSpec(a.shape, lambda bb, i: (0,) * a.ndim)
    y, st = pl.pallas_call(
        _ffn_prompt_kernel,
        out_shape=[jax.ShapeDtypeStruct((batch * seq, D_MODEL), F32),
                   jax.ShapeDtypeStruct((batch, SUBLANE, 2 * D_FF), F32)],
        grid=(batch, nt),
        in_specs=[pl.BlockSpec((tm, D_MODEL), lambda bb, i: (bb * nt + i, 0)),
                  full(w["w_up"]), full(w["conv_w"]), full(w["conv_b"]), full(w["w_down"]), full(g), full(b)],
        out_specs=[pl.BlockSpec((tm, D_MODEL), lambda bb, i: (bb * nt + i, 0)),
                   pl.BlockSpec((1, SUBLANE, 2 * D_FF), lambda bb, i: (bb, 0, 0))],
        scratch_shapes=[pltpu.VMEM((SUBLANE, 2 * D_FF), F32),
                        pltpu.VMEM((SUBLANE + tm, FFN_CW), F32), pltpu.VMEM((SUBLANE + tm, FFN_CW), F32),
                        pltpu.VMEM((tm, D_FF), BF16)],
        compiler_params=_cparams(("arbitrary", "arbitrary")),
        name="ffn_prompt",
    )(x, w["w_up"], w["conv_w"], w["conv_b"], w["w_down"], g, b)
    return y, st[:, SUBLANE - (CONV_W - 1):, :]


def _ffn_sample_kernel(x_ref, st_ref, wup_ref, cw_ref, cb_ref, wdn_ref, g_ref, b_ref, y_ref, ns_ref, act_ref):
    tm = x_ref.shape[0]
    w2 = 2 * D_FF

    @pl.when(pl.program_id(0) == 0)
    def _():
        ns_ref[...] = st_ref[...]

    x = x_ref[...]
    xb = x.astype(BF16)

    def conv(col0):
        cols = slice(col0, col0 + FFN_CW)
        cols1 = slice(w2 + col0, w2 + col0 + FFN_CW)
        u = jnp.dot(xb, wup_ref[:, cols], preferred_element_type=F32)
        p2 = ns_ref[:, cols]
        p1 = ns_ref[:, cols1]
        ns_ref[:, cols] = p1
        ns_ref[:, cols1] = u
        return cb_ref[:, cols] + cw_ref[0:1, cols] * p2 + cw_ref[1:2, cols] * p1 + cw_ref[2:3, cols] * u

    for c in range(D_FF // FFN_CW):
        act = _gelu(conv(c * FFN_CW)) * conv(D_FF + c * FFN_CW)
        act_ref[:, c * FFN_CW:(c + 1) * FFN_CW] = act.astype(BF16)
    down = jnp.dot(act_ref[...], wdn_ref[...], preferred_element_type=F32)
    y_ref[...] = _layer_norm(ALPHA * x + down, g_ref[...], b_ref[...])


def _ffn_sample(x, state, w, g, b, nb, n_new):
    assert CONV_W == 3
    full = lambda a: pl.BlockSpec(a.shape, lambda t: (0,) * a.ndim)
    st = state.reshape(nb, (CONV_W - 1) * 2 * D_FF)
    y, ns = pl.pallas_call(
        _ffn_sample_kernel,
        out_shape=[jax.ShapeDtypeStruct((n_new * nb, D_MODEL), F32), jax.ShapeDtypeStruct(st.shape, F32)],
        grid=(n_new,),
        in_specs=[pl.BlockSpec((nb, D_MODEL), lambda t: (t, 0)), full(st),
                  full(w["w_up"]), full(w["conv_w"]), full(w["conv_b"]), full(w["w_down"]), full(g), full(b)],
        out_specs=[pl.BlockSpec((nb, D_MODEL), lambda t: (t, 0)), full(st)],
        scratch_shapes=[pltpu.VMEM((nb, D_FF), BF16)],
        compiler_params=_cparams(("arbitrary",)),
        name="ffn_sample",
    )(x, st, w["w_up"], w["conv_w"], w["conv_b"], w["w_down"], g, b)
    return y, ns.reshape(nb, CONV_W - 1, 2 * D_FF)


def _attn_weights(w_in, q_norm, w_q_b, kv_norm, w_kv_b, w_out):
    o1 = Q_RANK
    o2 = o1 + KV_RANK
    o3 = o2 + ROPE_DIM
    kpe = w_in[:, o2:o3]
    w_in_r = jnp.concatenate([w_in[:, :o2], w_in[:, o3:]] + [kpe] * (LANE // ROPE_DIM), axis=1).astype(BF16)
    assert w_in_r.shape[1] == W_IN_COLS
    qb = w_q_b.reshape(Q_RANK, MLA_HEADS, NOPE_DIM + ROPE_DIM)
    nope = jnp.pad(qb[:, :, :NOPE_DIM], ((0, 0), (0, 0), (0, LANE - NOPE_DIM))).reshape(Q_RANK, MLA_HEADS * LANE)
    ropew = qb[:, :, NOPE_DIM:].reshape(Q_RANK, MLA_HEADS * ROPE_DIM)
    w_q_b_r = jnp.concatenate([nope, ropew], axis=1).astype(BF16)
    wk = jnp.transpose(w_kv_b[:, :, :NOPE_DIM], (1, 2, 0))
    wk = jnp.pad(wk, ((0, 0), (0, LANE - NOPE_DIM), (0, 0))).astype(BF16)
    wv = jnp.transpose(w_kv_b[:, :, NOPE_DIM:], (1, 0, 2))
    wv_lo = jnp.pad(wv, ((0, 0), (0, 0), (0, LANE - V_DIM)))
    wv_hi = jnp.pad(wv, ((0, 0), (0, 0), (LANE - V_DIM, 0)))
    odd = (jnp.arange(MLA_HEADS) % 2 == 1)[:, None, None]
    wvp = jnp.where(odd, wv_hi, wv_lo).astype(BF16)
    return dict(w_in=w_in_r, q_norm=q_norm.reshape(1, Q_RANK), w_q_b=w_q_b_r, wk=wk,
                kv_norm=kv_norm.reshape(1, KV_RANK), wv=wvp, w_out=w_out.astype(BF16))


def _ffn_weights(w_up, conv_w, conv_b, w_down):
    return dict(w_up=w_up.astype(BF16), conv_w=conv_w, conv_b=conv_b.reshape(1, 2 * D_FF), w_down=w_down.astype(BF16))


def _to_time_major(a):
    return jnp.swapaxes(a, 0, 1).reshape((a.shape[0] * a.shape[1],) + a.shape[2:])


def _from_time_major(a, nb):
    return jnp.swapaxes(a.reshape((a.shape[0] // nb, nb) + a.shape[1:]), 0, 1)


def kernel(x_prompt, x_sample, cache_mla_latent, cache_mla_rope, cache_moba_k, cache_moba_v, state_s5_re, state_s5_im, state_ffn_conv, page_table, ln_mix_g, ln_mix_b, ln_ffn_g, ln_ffn_b, att_w_in, mla_q_norm, mla_w_q_b, mla_kv_norm, mla_w_kv_b, att_w_out, s5_a_re, s5_a_im, s5_log_dt, s5_b_re, s5_b_im, s5_c_re, s5_c_im, s5_d, s5_w_glu, ffn_w_up, ffn_conv_w, ffn_conv_b, ffn_w_down):
    batch, seq, _ = x_prompt.shape
    nb, n_new, _ = x_sample.shape
    past_len = page_table.shape[1] * PAGE_SIZE
    depth = ln_mix_g.shape[0]
    assert seq % MLA_TK == 0 and nb % SUBLANE == 0 and (nb * n_new) % ROW_TILE == 0

    xp = x_prompt.reshape(batch * seq, D_MODEL)
    xs = _to_time_major(x_sample)
    tabs_p = _rope_tables(seq, 0)
    tabs_s = [jnp.repeat(t[:n_new], nb, axis=0) for t in _rope_tables(SUBLANE * pl.cdiv(n_new, SUBLANE), past_len)]
    row = lambda v: v.reshape(1, -1)
    pool_rope_t = jnp.swapaxes(cache_mla_rope, 2, 3)

    def tokens_minor(pool):
        return jnp.transpose(pool, (0, 1, 3, 4, 2)).reshape(pool.shape[0], pool.shape[1], MOBA_WIDTH, PAGE_SIZE)

    pool_kt, pool_vt = tokens_minor(cache_moba_k), tokens_minor(cache_moba_v)

    outs_p ={k: [] for k in ("lat", "rope", "mk", "mv", "sre", "sim", "conv")}
    outs_s = {k: [] for k in ("lat", "rope", "mk", "mv", "sre", "sim", "conv")}
    for l in range(depth):
        g_mix, b_mix = row(ln_mix_g[l]), row(ln_mix_b[l])
        if l % 2 == 0:
            a = l // 2
            w = _attn_weights(att_w_in[a], mla_q_norm[a], mla_w_q_b[a], mla_kv_norm[a], mla_w_kv_b[a], att_w_out[a])
            qcat, kcat, lat, rp, qm, km, kaug, vm, vmb, kmean = _attn_proj(xp, w, tabs_p, seq)
            o_mla = _mla_prompt(qcat, kcat, batch, seq)
            qaug = _moba_select(qm, kmean, batch, seq)
            o_moba = _moba_prompt(qaug, kaug, vmb, batch, seq)
            xp = _attn_out(xp, o_mla, o_moba, w, g_mix, b_mix)
            outs_p["lat"].append(lat.reshape(batch, seq, KV_RANK))
            outs_p["rope"].append(rp.reshape(batch, seq, ROPE_DIM))
            outs_p["mk"].append(km.reshape(batch, seq, MOBA_HEADS, MOBA_DIM))
            outs_p["mv"].append(vm.reshape(batch, seq, MOBA_HEADS, MOBA_DIM))
            qcat, _, lat, rp, qm, km, _, vm, _, _ = _attn_proj(xs, w, tabs_s, nb * n_new)
            lat_b, rp_b = _from_time_major(lat, nb), _from_time_major(rp, nb)
            km_b, vm_b = _from_time_major(km, nb), _from_time_major(vm, nb)
            qm_b = _from_time_major(qm, nb)
            qc = qcat.reshape(n_new, nb // LANE, MLA_HEADS, LANE, 2 * LANE)
            qc = jnp.transpose(qc, (1, 3, 0, 2, 4)).reshape(nb, n_new * MLA_HEADS, 2 * LANE)
            qlat = qc[:, :, :KV_RANK]
            qr4 = qc[:, :, KV_RANK:].reshape(nb, n_new, MLA_HEADS, LANE // ROPE_DIM, ROPE_DIM)
            grp = (jnp.arange(MLA_HEADS) % (LANE // ROPE_DIM))
            qrope = jnp.take_along_axis(qr4, grp[None, None, :, None, None], axis=3)[:, :, :, 0, :]
            qrope = qrope.reshape(nb, n_new * MLA_HEADS, ROPE_DIM)
            o_mla = _mla_sample(page_table, qlat, qrope, lat_b, rp_b.astype(BF16), cache_mla_latent, pool_rope_t, a)
            q4 = qm_b.reshape(nb, n_new, MOBA_HEADS, MOBA_DIM)
            eye = jnp.eye(MOBA_HEADS, dtype=F32)
            qbdf = jnp.einsum("bthd,hk->bthkd", q4, eye).reshape(nb, n_new * MOBA_HEADS, MOBA_WIDTH)
            o_moba = _moba_sample(page_table, qbdf.astype(BF16), qbdf, km_b.astype(BF16), vm_b.astype(BF16),
                                  pool_kt, pool_vt, a)
            om = o_mla.reshape(nb // LANE, LANE, n_new, MLA_HEADS, KV_RANK)
            om = jnp.transpose(om, (2, 0, 3, 1, 4)).reshape(n_new * nb // LANE, MLA_HEADS, LANE, KV_RANK)
            xs = _attn_out(xs, om, _to_time_major(o_moba), w, g_mix, b_mix)
            outs_s["lat"].append(lat_b)
            outs_s["rope"].append(rp_b)
            outs_s["mk"].append(km_b.reshape(nb, n_new, MOBA_HEADS, MOBA_DIM))
            outs_s["mv"].append(vm_b.reshape(nb, n_new, MOBA_HEADS, MOBA_DIM))
        else:
            s = l // 2
            sp = _s5_params(s5_a_re[s], s5_a_im[s], s5_log_dt[s], s5_b_re[s], s5_b_im[s], s5_c_re[s], s5_c_im[s])
            d, wglu = row(s5_d[s]), s5_w_glu[s].astype(BF16)
            xp, hre, him = _s5_prompt(xp, sp, d, wglu, g_mix, b_mix, batch, seq)
            outs_p["sre"].append(hre)
            outs_p["sim"].append(him)
            xs, hre, him = _s5_sample(xs, state_s5_re[s], state_s5_im[s], sp, d, wglu, g_mix, b_mix, nb, n_new)
            outs_s["sre"].append(hre)
            outs_s["sim"].append(him)
        fw = _ffn_weights(ffn_w_up[l], ffn_conv_w[l], ffn_conv_b[l], ffn_w_down[l])
        g_ffn, b_ffn = row(ln_ffn_g[l]), row(ln_ffn_b[l])
        xp, cp = _ffn_prompt(xp, fw, g_ffn, b_ffn, batch, seq)
        xs, cs = _ffn_sample(xs, state_ffn_conv[l], fw, g_ffn, b_ffn, nb, n_new)
        outs_p["conv"].append(cp)
        outs_s["conv"].append(cs)

    keys = ("lat", "rope", "mk", "mv", "sre", "sim", "conv")
    return ((xp.reshape(batch, seq, D_MODEL), _from_time_major(xs, nb))
            + tuple(jnp.stack(outs_p[k]) for k in keys) + tuple(jnp.stack(outs_s[k]) for k in keys))
```

```python
import functools
import math

import jax
import jax.numpy as jnp
from jax import lax
from jax.experimental import pallas as pl
from jax.experimental.pallas import tpu as pltpu

F32 = jnp.float32
BF16 = jnp.bfloat16

D_MODEL = 1024
PAGE_SIZE = 128
MLA_HEADS = 8
Q_RANK = 256
KV_RANK = 128
NOPE_DIM = 64
ROPE_DIM = 32
V_DIM = 64
MLA_SCALE = (NOPE_DIM + ROPE_DIM) ** -0.5
MOBA_HEADS = 8
MOBA_DIM = 64
MOBA_WIDTH = MOBA_HEADS * MOBA_DIM
MOBA_BLOCK = 256
MOBA_TOPK = 3
MOBA_SCALE = MOBA_DIM ** -0.5
ROPE_THETA = 10000.0
SSM_GROUP = 16
SSM_GROUPS = D_MODEL // SSM_GROUP
SSM_STATE = 64
SSM_CH = SSM_GROUPS * SSM_STATE
D_FF = 2816
CONV_W = 3
DEPTH = 4
ALPHA = (2 * DEPTH) ** 0.25
LN_EPS = 1e-5
RMS_EPS = 1e-6

LANE = 128
SUBLANE = 8
NEG = -1e30
LOG2E = math.log2(math.e)
ROW_TILE = 256
MLA_TQ = 128
MLA_TK = 1024
MOBA_GROUP = 4
S5_CHUNKS = 4
S5_CIN = D_MODEL // S5_CHUNKS
S5_CST = SSM_CH // S5_CHUNKS
FFN_CW = 256
PAGES_PER_STEP = 16
W_IN_COLS = Q_RANK + KV_RANK + 3 * MOBA_WIDTH + LANE
VMEM_LIMIT = 56 * 1024 * 1024


def _cparams(sem):
    return pltpu.CompilerParams(dimension_semantics=sem, vmem_limit_bytes=VMEM_LIMIT)


def _bdot(a, b):
    return jnp.dot(a.astype(BF16), b.astype(BF16), preferred_element_type=F32)


def _bdot_nt(a, b):
    return lax.dot_general(a.astype(BF16), b.astype(BF16), (((1,), (1,)), ((), ())),
                           preferred_element_type=F32)


def _layer_norm(v, g, b):
    mu = jnp.mean(v, axis=-1, keepdims=True)
    d = v - mu
    var = jnp.mean(d * d, axis=-1, keepdims=True)
    return d * lax.rsqrt(var + LN_EPS) * g + b


def _rms_norm(v, g):
    return v * lax.rsqrt(jnp.mean(v * v, axis=-1, keepdims=True) + RMS_EPS) * g


def _gelu(v):
    return 0.5 * v * (1.0 + jnp.tanh(math.sqrt(2.0 / math.pi) * (v + 0.044715 * (v * v * v))))


def _sigmoid(v):
    return 1.0 / (1.0 + jnp.exp(-v))


def _swap_halves(v, half):
    w = v.shape[-1]
    lane = lax.broadcasted_iota(jnp.int32, v.shape, 1)
    fwd = pltpu.roll(v, w - half, axis=1)
    bwd = pltpu.roll(v, half, axis=1)
    return jnp.where((lane % (2 * half)) < half, fwd, bwd)


def _rope(v, cos, sin_signed, half):
    outs = []
    for c in range(v.shape[-1] // LANE):
        vc = v[:, c * LANE:(c + 1) * LANE]
        outs.append(vc * cos + _swap_halves(vc, half) * sin_signed)
    return outs[0] if len(outs) == 1 else jnp.concatenate(outs, axis=-1)


def _rope_table_kernel(inv64_ref, sg64_ref, inv32_ref, sg32_ref, c64_ref, s64_ref, c32_ref, s32_ref, *, offset):
    rows = c64_ref.shape[0]
    base = pl.program_id(0) * rows + offset
    pos = (base + lax.broadcasted_iota(jnp.int32, (rows, LANE), 0)).astype(F32)
    a64 = pos * inv64_ref[...]
    c64_ref[...] = jnp.cos(a64)
    s64_ref[...] = jnp.sin(a64) * sg64_ref[...]
    a32 = pos * inv32_ref[...]
    c32_ref[...] = jnp.cos(a32)
    s32_ref[...] = jnp.sin(a32) * sg32_ref[...]


def _rope_tables(n_rows, offset):
    def lane_consts(d):
        half = d // 2
        inv = ROPE_THETA ** (-jnp.arange(half, dtype=F32) * 2.0 / d)
        l = jnp.arange(LANE)
        return (inv[(l % d) % half].reshape(1, LANE),
                jnp.where((l % d) < half, -1.0, 1.0).astype(F32).reshape(1, LANE))
    inv64, sg64 = lane_consts(MOBA_DIM)
    inv32, sg32 = lane_consts(ROPE_DIM)
    rows = min(n_rows, 512)
    assert n_rows % rows == 0 and rows % SUBLANE == 0
    const = pl.BlockSpec((1, LANE), lambda i: (0, 0))
    out = pl.BlockSpec((rows, LANE), lambda i: (i, 0))
    return pl.pallas_call(
        functools.partial(_rope_table_kernel, offset=offset),
        out_shape=[jax.ShapeDtypeStruct((n_rows, LANE), F32)] * 4,
        grid=(n_rows // rows,),
        in_specs=[const] * 4,
        out_specs=[out] * 4,
        compiler_params=_cparams(("arbitrary",)),
        name="rope_tables",
    )(inv64, sg64, inv32, sg32)


def _proj_kernel(x_ref, win_ref, qn_ref, wqb_ref, wk_ref, kvn_ref, c64_ref, s64_ref, c32_ref, s32_ref,
                 qcat_ref, kcat_ref, lat_ref, rope_ref, qm_ref, km_ref, kaug_ref, vm_ref, vmb_ref, kmean_ref,
                 *, blocks_per_seq):
    tm = x_ref.shape[0]
    nsub = tm // LANE
    c64, s64, c32, s32 = c64_ref[...], s64_ref[...], c32_ref[...], s32_ref[...]
    h = _bdot(x_ref[...], win_ref[...])
    o_kv = Q_RANK
    o_qm = o_kv + KV_RANK
    o_km = o_qm + MOBA_WIDTH
    o_vm = o_km + MOBA_WIDTH
    o_pe = o_vm + MOBA_WIDTH

    qn = _rms_norm(h[:, :Q_RANK], qn_ref[...])
    q = _bdot(qn, wqb_ref[...])
    o_qr = MLA_HEADS * LANE
    qr = _rope(q[:, o_qr:o_qr + MLA_HEADS * ROPE_DIM], c32, s32, ROPE_DIM // 2)
    lane = lax.broadcasted_iota(jnp.int32, (tm, LANE), 1)
    heads_per_chunk = LANE // ROPE_DIM
    for hd in range(MLA_HEADS):
        ql = _bdot(q[:, hd * LANE:(hd + 1) * LANE], wk_ref[hd]).astype(BF16)
        grp = hd % heads_per_chunk
        chunk = qr[:, (hd // heads_per_chunk) * LANE:(hd // heads_per_chunk + 1) * LANE]
        qrh = jnp.where((lane >= grp * ROPE_DIM) & (lane < (grp + 1) * ROPE_DIM), chunk, 0.0).astype(BF16)
        for sb in range(nsub):
            qcat_ref[sb, hd, :, 0:LANE] = ql[sb * LANE:(sb + 1) * LANE]
            qcat_ref[sb, hd, :, LANE:2 * LANE] = qrh[sb * LANE:(sb + 1) * LANE]

    lat = _rms_norm(h[:, o_kv:o_kv + KV_RANK], kvn_ref[...])
    lat_ref[...] = lat
    kpe = _rope(h[:, o_pe:o_pe + LANE], c32, s32, ROPE_DIM // 2)
    rope_ref[...] = kpe[:, :ROPE_DIM]
    kcat_ref[:, 0:KV_RANK] = lat.astype(BF16)
    kcat_ref[:, KV_RANK:KV_RANK + LANE] = kpe.astype(BF16)

    qm_ref[...] = _rope(h[:, o_qm:o_qm + MOBA_WIDTH], c64, s64, MOBA_DIM // 2)
    km = _rope(h[:, o_km:o_km + MOBA_WIDTH], c64, s64, MOBA_DIM // 2)
    km_ref[...] = km
    kmean_ref[0] = jnp.mean(km, axis=0, keepdims=True)
    vm = h[:, o_vm:o_vm + MOBA_WIDTH]
    vm_ref[...] = vm
    vmb_ref[...] = vm.astype(BF16)
    blk = pl.program_id(0) % blocks_per_seq
    onehot = jnp.where(lane == MOBA_DIM + blk, 1.0, 0.0)
    for g in range(MOBA_HEADS // 2):
        ch = km[:, g * LANE:(g + 1) * LANE]
        kaug_ref[:, (2 * g) * LANE:(2 * g + 1) * LANE] = jnp.where(lane < MOBA_DIM, ch, onehot).astype(BF16)
        chs = pltpu.roll(ch, MOBA_DIM, axis=1)
        kaug_ref[:, (2 * g + 1) * LANE:(2 * g + 2) * LANE] = jnp.where(lane < MOBA_DIM, chs, onehot).astype(BF16)


def _attn_proj(x, w, tabs, seq_rows):
    n = x.shape[0]
    tm = ROW_TILE
    assert tm == MOBA_BLOCK and n % tm == 0 and seq_rows % tm == 0
    tiles_per_seq = seq_rows // tm
    nsub = tm // LANE
    c64, s64, c32, s32 = tabs
    row = lambda width: pl.BlockSpec((tm, width), lambda i: (i, 0))
    tab = pl.BlockSpec((tm, LANE), lambda i: (i % tiles_per_seq, 0))
    full = lambda a: pl.BlockSpec(a.shape, lambda i: (0,) * a.ndim)
    outs = pl.pallas_call(
        functools.partial(_proj_kernel, blocks_per_seq=tiles_per_seq),
        out_shape=[
            jax.ShapeDtypeStruct((n // LANE, MLA_HEADS, LANE, 2 * LANE), BF16),
            jax.ShapeDtypeStruct((n, 2 * LANE), BF16),
            jax.ShapeDtypeStruct((n, KV_RANK), F32),
            jax.ShapeDtypeStruct((n, ROPE_DIM), F32),
            jax.ShapeDtypeStruct((n, MOBA_WIDTH), F32),
            jax.ShapeDtypeStruct((n, MOBA_WIDTH), F32),
            jax.ShapeDtypeStruct((n, MOBA_HEADS * LANE), BF16),
            jax.ShapeDtypeStruct((n, MOBA_WIDTH), F32),
            jax.ShapeDtypeStruct((n, MOBA_WIDTH), BF16),
            jax.ShapeDtypeStruct((n // tm, 1, MOBA_WIDTH), F32),
        ],
        grid=(n // tm,),
        in_specs=[row(D_MODEL), full(w["w_in"]), full(w["q_norm"]), full(w["w_q_b"]), full(w["wk"]),
                  full(w["kv_norm"]), tab, tab, tab, tab],
        out_specs=[
            pl.BlockSpec((nsub, MLA_HEADS, LANE, 2 * LANE), lambda i: (i, 0, 0, 0)),
            row(2 * LANE), row(KV_RANK), row(ROPE_DIM), row(MOBA_WIDTH), row(MOBA_WIDTH),
            row(MOBA_HEADS * LANE), row(MOBA_WIDTH), row(MOBA_WIDTH),
            pl.BlockSpec((1, 1, MOBA_WIDTH), lambda i: (i, 0, 0)),
        ],
        compiler_params=_cparams(("arbitrary",)),
        name="attn_proj",
    )(x, w["w_in"], w["q_norm"], w["w_q_b"], w["wk"], w["kv_norm"], c64, s64, c32, s32)
    return outs


def _mla_prompt_kernel(q_ref, kcat_ref, o_ref, m_ref, l_ref, acc_ref):
    qi = pl.program_id(1)
    rows = MLA_HEADS * MLA_TQ
    q = q_ref[0].reshape(rows, 2 * LANE)
    cexp = MLA_SCALE * LOG2E

    def scores(kt):
        k = kcat_ref[pl.ds(pl.multiple_of(kt * MLA_TK, MLA_TK), MLA_TK), :]
        return _bdot_nt(q, k), k[:, :KV_RANK]

    kd = (qi * MLA_TQ) // MLA_TK
    s, v = scores(kd)
    col = kd * MLA_TK + lax.broadcasted_iota(jnp.int32, (rows, MLA_TK), 1)
    rowpos = qi * MLA_TQ + lax.broadcasted_iota(jnp.int32, (rows, MLA_TK), 0) % MLA_TQ
    s = jnp.where(col <= rowpos, s, NEG)
    m = jnp.max(s, axis=1, keepdims=True)
    p = jnp.exp2((s - m) * cexp)
    m_ref[...] = m
    l_ref[...] = jnp.sum(p, axis=1, keepdims=True)
    acc_ref[...] = jnp.dot(p.astype(BF16), v, preferred_element_type=F32)

    def body(kt, carry):
        s, v = scores(kt)
        m_old = m_ref[...]
        m_new = jnp.maximum(m_old, jnp.max(s, axis=1, keepdims=True))
        alpha = jnp.exp2((m_old - m_new) * cexp)
        p = jnp.exp2((s - m_new) * cexp)
        m_ref[...] = m_new
        l_ref[...] = l_ref[...] * alpha + jnp.sum(p, axis=1, keepdims=True)
        acc_ref[...] = acc_ref[...] * alpha + jnp.dot(p.astype(BF16), v, preferred_element_type=F32)
        return carry

    lax.fori_loop(0, kd, body, 0)
    o = (acc_ref[...] / l_ref[...]).astype(BF16)
    o_ref[0] = o.reshape(MLA_HEADS, MLA_TQ, KV_RANK)


def _mla_prompt(qcat, kcat, batch, seq):
    assert MLA_TQ == LANE and seq % MLA_TK == 0 and MLA_TK % MLA_TQ == 0
    nq = seq // MLA_TQ
    rows = MLA_HEADS * MLA_TQ
    return pl.pallas_call(
        _mla_prompt_kernel,
        out_shape=jax.ShapeDtypeStruct((batch * nq, MLA_HEADS, MLA_TQ, KV_RANK), BF16),
        grid=(batch, nq),
        in_specs=[pl.BlockSpec((1, MLA_HEADS, MLA_TQ, 2 * LANE), lambda b, i: (b * nq + i, 0, 0, 0)),
                  pl.BlockSpec((seq, 2 * LANE), lambda b, i: (b, 0))],
        out_specs=pl.BlockSpec((1, MLA_HEADS, MLA_TQ, KV_RANK), lambda b, i: (b * nq + i, 0, 0, 0)),
        scratch_shapes=[pltpu.VMEM((rows, 1), F32), pltpu.VMEM((rows, 1), F32), pltpu.VMEM((rows, KV_RANK), F32)],
        compiler_params=_cparams(("arbitrary", "arbitrary")),
        name="mla_prompt",
    )(qcat, kcat)


def _class_reduce(v, op):
    for sh in (8, 16, 32, 64):
        v = op(v, pltpu.roll(v, sh, axis=1))
    return v


def _moba_select_kernel(qm_ref, kmbd_ref, perm_ref, qaug_ref, *, blocks_per_seq):
    tm = qm_ref.shape[0]
    qm = qm_ref[...]
    gate = jnp.dot(qm, kmbd_ref[...], precision=lax.Precision.HIGHEST, preferred_element_type=F32)
    lane = lax.broadcasted_iota(jnp.int32, (tm, LANE), 1)
    nblk = lane // MOBA_HEADS
    own = pl.program_id(0) % blocks_per_seq
    past = nblk < own
    g = jnp.where(past, gate, -jnp.inf)
    picked = jnp.zeros((tm, LANE), F32)
    for _ in range(MOBA_TOPK):
        top = _class_reduce(g, jnp.maximum)
        first = _class_reduce(jnp.where(g == top, nblk, LANE), jnp.minimum)
        hit = nblk == first
        picked = picked + jnp.where(hit & (top > -jnp.inf), 1.0, 0.0)
        g = jnp.where(hit, -jnp.inf, g)
    masked = jnp.where(past & (picked > 0.5), 0.0, 1.0)
    bias = jnp.dot(masked.astype(BF16), perm_ref[...], preferred_element_type=F32) * NEG
    for gp in range(MOBA_HEADS // 2):
        ch = qm[:, gp * LANE:(gp + 1) * LANE]
        for e, src in ((0, ch), (1, pltpu.roll(ch, MOBA_DIM, axis=1))):
            hd = 2 * gp + e
            slot = jnp.where(lane < MOBA_DIM, src, 0.0) + bias[:, hd * LANE:(hd + 1) * LANE]
            qaug_ref[:, hd * LANE:(hd + 1) * LANE] = slot.astype(BF16)


def _moba_select(qm, kmean, batch, seq):
    n = qm.shape[0]
    tm = ROW_TILE
    nblk = seq // MOBA_BLOCK
    assert tm == MOBA_BLOCK and nblk * MOBA_HEADS <= LANE
    km4 = kmean.reshape(batch, nblk, MOBA_HEADS, MOBA_DIM)
    eye = jnp.eye(MOBA_HEADS, dtype=F32)
    kmbd = jnp.einsum("bnhd,hk->bhdnk", km4, eye).reshape(batch, MOBA_WIDTH, nblk * MOBA_HEADS)
    kmbd = jnp.pad(kmbd, ((0, 0), (0, 0), (0, LANE - nblk * MOBA_HEADS)))
    src = jnp.arange(LANE)
    dst = (src % MOBA_HEADS) * LANE + MOBA_DIM + src // MOBA_HEADS
    perm = (jnp.arange(MOBA_HEADS * LANE)[None, :] == dst[:, None]).astype(BF16)
    return pl.pallas_call(
        functools.partial(_moba_select_kernel, blocks_per_seq=nblk),
        out_shape=jax.ShapeDtypeStruct((n, MOBA_HEADS * LANE), BF16),
        grid=(n // tm,),
        in_specs=[pl.BlockSpec((tm, MOBA_WIDTH), lambda i: (i, 0)),
                  pl.BlockSpec((None, MOBA_WIDTH, LANE), lambda i: (i // nblk, 0, 0)),
                  pl.BlockSpec(perm.shape, lambda i: (0, 0))],
        out_specs=pl.BlockSpec((tm, MOBA_HEADS * LANE), lambda i: (i, 0)),
        compiler_params=_cparams(("arbitrary",)),
        name="moba_select",
    )(qm, kmbd, perm)


def _moba_prompt_kernel(qaug_ref, kaug_ref, v_ref, o_ref, m_ref, l_ref, acc_ref):
    own = pl.program_id(1)
    tq = qaug_ref.shape[0]
    cexp = MOBA_SCALE * LOG2E
    lane = lax.broadcasted_iota(jnp.int32, (tq, LANE), 1)

    def scores(r0, rows, hd, with_bias):
        q = qaug_ref[:, hd * LANE:(hd + 1) * LANE]
        if not with_bias:
            q = jnp.where(lane < MOBA_DIM, q.astype(F32), 0.0).astype(BF16)
        k = kaug_ref[pl.ds(r0, rows), hd * LANE:(hd + 1) * LANE]
        v = v_ref[pl.ds(r0, rows), (hd // 2) * LANE:(hd // 2 + 1) * LANE]
        return _bdot_nt(q, k), v

    tri = (lax.broadcasted_iota(jnp.int32, (tq, MOBA_BLOCK), 1)
           <= lax.broadcasted_iota(jnp.int32, (tq, MOBA_BLOCK), 0))
    for hd in range(MOBA_HEADS):
        s, v = scores(pl.multiple_of(own * MOBA_BLOCK, MOBA_BLOCK), MOBA_BLOCK, hd, False)
        s = jnp.where(tri, s, NEG)
        m = jnp.max(s, axis=1, keepdims=True)
        p = jnp.exp2((s - m) * cexp)
        m_ref[hd] = m
        l_ref[hd] = jnp.sum(p, axis=1, keepdims=True)
        acc_ref[hd] = jnp.dot(p.astype(BF16), v, preferred_element_type=F32)

    group_rows = MOBA_GROUP * MOBA_BLOCK

    def body(gi, carry):
        for hd in range(MOBA_HEADS):
            s, v = scores(pl.multiple_of(gi * group_rows, group_rows), group_rows, hd, True)
            m_old = m_ref[hd]
            m_new = jnp.maximum(m_old, jnp.max(s, axis=1, keepdims=True))
            alpha = jnp.exp2((m_old - m_new) * cexp)
            p = jnp.exp2((s - m_new) * cexp)
            m_ref[hd] = m_new
            l_ref[hd] = l_ref[hd] * alpha + jnp.sum(p, axis=1, keepdims=True)
            acc_ref[hd] = acc_ref[hd] * alpha + jnp.dot(p.astype(BF16), v, preferred_element_type=F32)
        return carry

    lax.fori_loop(0, (own + MOBA_GROUP - 1) // MOBA_GROUP, body, 0)
    for gp in range(MOBA_HEADS // 2):
        lo = acc_ref[2 * gp] / l_ref[2 * gp]
        hi = acc_ref[2 * gp + 1] / l_ref[2 * gp + 1]
        o_ref[:, gp * LANE:(gp + 1) * LANE] = jnp.where(lane < MOBA_DIM, lo, hi).astype(BF16)


def _moba_prompt(qaug, kaug, vmb, batch, seq):
    tq = MOBA_BLOCK
    nq = seq // tq
    assert seq % (MOBA_GROUP * MOBA_BLOCK) == 0
    return pl.pallas_call(
        _moba_prompt_kernel,
        out_shape=jax.ShapeDtypeStruct((batch * seq, MOBA_WIDTH), BF16),
        grid=(batch, nq),
        in_specs=[pl.BlockSpec((tq, MOBA_HEADS * LANE), lambda b, i: (b * nq + i, 0)),
                  pl.BlockSpec((seq, MOBA_HEADS * LANE), lambda b, i: (b, 0)),
                  pl.BlockSpec((seq, MOBA_WIDTH), lambda b, i: (b, 0))],
        out_specs=pl.BlockSpec((tq, MOBA_WIDTH), lambda b, i: (b * nq + i, 0)),
        scratch_shapes=[pltpu.VMEM((MOBA_HEADS, tq, 1), F32), pltpu.VMEM((MOBA_HEADS, tq, 1), F32),
                        pltpu.VMEM((MOBA_HEADS, tq, LANE), F32)],
        compiler_params=_cparams(("arbitrary", "arbitrary")),
        name="moba_prompt",
    )(qaug, kaug, vmb)


def _attn_out_kernel(x_ref, omla_ref, omoba_ref, wv_ref, wout_ref, g_ref, b_ref, y_ref):
    tm = x_ref.shape[0]
    x = x_ref[...]
    mix = _bdot(omoba_ref[...], wout_ref[MLA_HEADS * V_DIM:, :])
    for gp in range(MLA_HEADS // 2):
        v2 = jnp.zeros((tm, LANE), F32)
        for e in range(2):
            hd = 2 * gp + e
            o = omla_ref[:, hd].reshape(tm, KV_RANK)
            v2 = v2 + jnp.dot(o, wv_ref[hd], preferred_element_type=F32)
        mix = mix + _bdot(v2, wout_ref[gp * LANE:(gp + 1) * LANE, :])
    y_ref[...] = _layer_norm(ALPHA * x + mix, g_ref[...], b_ref[...])


def _attn_out(x, omla, omoba, w, g, b):
    n = x.shape[0]
    tm = ROW_TILE
    nsub = tm // LANE
    full = lambda a: pl.BlockSpec(a.shape, lambda i: (0,) * a.ndim)
    return pl.pallas_call(
        _attn_out_kernel,
        out_shape=jax.ShapeDtypeStruct((n, D_MODEL), F32),
        grid=(n // tm,),
        in_specs=[pl.BlockSpec((tm, D_MODEL), lambda i: (i, 0)),
                  pl.BlockSpec((nsub, MLA_HEADS, LANE, KV_RANK), lambda i: (i, 0, 0, 0)),
                  pl.BlockSpec((tm, MOBA_WIDTH), lambda i: (i, 0)),
                  full(w["wv"]), full(w["w_out"]), full(g), full(b)],
        out_specs=pl.BlockSpec((tm, D_MODEL), lambda i: (i, 0)),
        compiler_params=_cparams(("arbitrary",)),
        name="attn_out",
    )(x, omla, omoba, w["wv"], w["w_out"], g, b)


def _mla_sample_kernel(pt_ref, qlat_ref, qrope_ref, latn_ref, ropen_ref, *rest, n_new):
    pps = PAGES_PER_STEP
    lat_refs, rope_refs = rest[:pps], rest[pps:2 * pps]
    o_ref, m_ref, l_ref, acc_ref = rest[2 * pps:]
    j = pl.program_id(1)
    cexp = MLA_SCALE * LOG2E
    ql = qlat_ref[0]
    qr = qrope_ref[0]
    rows = ql.shape[0]

    @pl.when(j == 0)
    def _():
        m_ref[...] = jnp.full((rows, 1), NEG, F32)
        l_ref[...] = jnp.zeros((rows, 1), F32)
        acc_ref[...] = jnp.zeros((rows, KV_RANK), F32)

    lat = jnp.concatenate([r[...] for r in lat_refs], axis=0).astype(BF16)
    rpt = jnp.concatenate([r[...] for r in rope_refs], axis=1)
    s = _bdot_nt(ql, lat) + _bdot(qr, rpt)
    m_old = m_ref[...]
    m_new = jnp.maximum(m_old, jnp.max(s, axis=1, keepdims=True))
    alpha = jnp.exp2((m_old - m_new) * cexp)
    p = jnp.exp2((s - m_new) * cexp)
    m_ref[...] = m_new
    l_ref[...] = l_ref[...] * alpha + jnp.sum(p, axis=1, keepdims=True)
    acc_ref[...] = acc_ref[...] * alpha + jnp.dot(p.astype(BF16), lat, preferred_element_type=F32)

    @pl.when(j == pl.num_programs(1) - 1)
    def _():
        latn = latn_ref[0].astype(BF16)
        s = _bdot_nt(ql, latn) + _bdot_nt(qr, ropen_ref[0])
        tq = lax.broadcasted_iota(jnp.int32, s.shape, 0) // MLA_HEADS
        tk = lax.broadcasted_iota(jnp.int32, s.shape, 1)
        s = jnp.where(tk <= tq, s, NEG)
        m_old = m_ref[...]
        m_new = jnp.maximum(m_old, jnp.max(s, axis=1, keepdims=True))
        alpha = jnp.exp2((m_old - m_new) * cexp)
        p = jnp.exp2((s - m_new) * cexp)
        l = l_ref[...] * alpha + jnp.sum(p, axis=1, keepdims=True)
        acc = acc_ref[...] * alpha + jnp.dot(p.astype(BF16), latn, preferred_element_type=F32)
        o_ref[0] = (acc / l).astype(BF16)


def _mla_sample(page_table, qlat, qrope, lat_new, rope_new, pool_lat, pool_rope_t, layer):
    nb, rows, _ = qlat.shape
    n_new = lat_new.shape[1]
    n_pages = page_table.shape[1]
    pps = PAGES_PER_STEP
    assert n_pages % pps == 0
    pt = page_table.reshape(-1)

    def page(page_rows, i):
        return pl.BlockSpec((None, None, page_rows, PAGE_SIZE),
                            lambda b, j, pt_ref: (layer, pt_ref[b * n_pages + j * pps + i], 0, 0))

    per_seq = lambda a: pl.BlockSpec((1,) + a.shape[1:], lambda b, j, pt_ref: (b, 0, 0))
    grid_spec = pltpu.PrefetchScalarGridSpec(
        num_scalar_prefetch=1,
        grid=(nb, n_pages // pps),
        in_specs=[per_seq(qlat), per_seq(qrope), per_seq(lat_new), per_seq(rope_new)]
                 + [page(PAGE_SIZE, i) for i in range(pps)] + [page(ROPE_DIM, i) for i in range(pps)],
        out_specs=pl.BlockSpec((1, rows, KV_RANK), lambda b, j, pt_ref: (b, 0, 0)),
        scratch_shapes=[pltpu.VMEM((rows, 1), F32), pltpu.VMEM((rows, 1), F32), pltpu.VMEM((rows, KV_RANK), F32)],
    )
    return pl.pallas_call(
        functools.partial(_mla_sample_kernel, n_new=n_new),
        out_shape=jax.ShapeDtypeStruct((nb, rows, KV_RANK), BF16),
        grid_spec=grid_spec,
        compiler_params=_cparams(("arbitrary", "arbitrary")),
        name="mla_sample",
    )(pt, qlat, qrope, lat_new, rope_new, *([pool_lat] * pps), *([pool_rope_t] * pps))


def _moba_sample_kernel(pt_ref, qbd_ref, qbdf_ref, kn_ref, vn_ref, expand_ref, *rest, n_new, n_past_blocks):
    pps = PAGES_PER_STEP
    k_refs, v_refs = rest[:pps], rest[pps:2 * pps]
    o_ref, s_ref, ksum_ref, p_ref, pn_ref, den_ref, acc_ref = rest[2 * pps:]
    phase = pl.program_id(1)
    j = pl.program_id(2)
    cexp = MOBA_SCALE * LOG2E
    qbd = qbd_ref[0]
    pages_per_block = MOBA_BLOCK // PAGE_SIZE

    @pl.when((phase == 0) & (j == 0))
    def _():
        ksum_ref[...] = jnp.zeros(ksum_ref.shape, F32)

    @pl.when(phase == 0)
    def _():
        lane = lax.broadcasted_iota(jnp.int32, ksum_ref.shape, 1)
        for i0 in range(0, pps, pages_per_block):
            ksum = None
            for i in range(i0, i0 + pages_per_block):
                kt = k_refs[i][...]
                col = pl.multiple_of((j * pps + i) * PAGE_SIZE, PAGE_SIZE)
                s_ref[:, pl.ds(col, PAGE_SIZE)] = _bdot(qbd, kt)
                ksum = kt if ksum is None else ksum + kt
            blk = (j * pps + i0) // pages_per_block
            ksum_ref[...] = jnp.where(lane == blk, jnp.sum(ksum, axis=1, keepdims=True), ksum_ref[...])

    @pl.when((phase == 1) & (j == 0))
    def _():
        kmean = ksum_ref[...] * (1.0 / MOBA_BLOCK)
        gate = jnp.dot(qbdf_ref[0], kmean, precision=lax.Precision.HIGHEST, preferred_element_type=F32)
        nidx = lax.broadcasted_iota(jnp.int32, gate.shape, 1)
        g = jnp.where(nidx < n_past_blocks, gate, -jnp.inf)
        nidx = nidx.astype(F32)
        picked = jnp.zeros(gate.shape, F32)
        for _ in range(MOBA_TOPK):
            top = jnp.max(g, axis=1, keepdims=True)
            first = jnp.min(jnp.where(g == top, nidx, float(LANE)), axis=1, keepdims=True)
            hit = nidx == first
            picked = picked + jnp.where(hit & (top > -jnp.inf), 1.0, 0.0)
            g = jnp.where(hit, -jnp.inf, g)
        chosen = jnp.dot(picked.astype(BF16), expand_ref[...], preferred_element_type=F32)
        s = jnp.where(chosen > 0.5, s_ref[...], NEG)
        sn = _bdot_nt(qbd, kn_ref[0])
        tq = lax.broadcasted_iota(jnp.int32, sn.shape, 0) // MOBA_HEADS
        tk = lax.broadcasted_iota(jnp.int32, sn.shape, 1)
        sn = jnp.where(tk <= tq, sn, NEG)
        m = jnp.maximum(jnp.max(s, axis=1, keepdims=True), jnp.max(sn, axis=1, keepdims=True))
        p = jnp.exp2((s - m) * cexp)
        pn = jnp.exp2((sn - m) * cexp)
        den_ref[...] = jnp.sum(p, axis=1, keepdims=True) + jnp.sum(pn, axis=1, keepdims=True)
        p_ref[...] = p.astype(BF16)
        pn_ref[...] = pn
        acc_ref[...] = jnp.zeros(acc_ref.shape, F32)

    @pl.when(phase == 1)
    def _():
        acc = acc_ref[...]
        for i in range(pps):
            col = pl.multiple_of((j * pps + i) * PAGE_SIZE, PAGE_SIZE)
            acc = acc + _bdot_nt(p_ref[:, pl.ds(col, PAGE_SIZE)], v_refs[i][...])
        acc_ref[...] = acc

    @pl.when((phase == 1) & (j == pl.num_programs(2) - 1))
    def _():
        o = (acc_ref[...] + _bdot(pn_ref[...], vn_ref[0])) / den_ref[...]
        hrow = lax.broadcasted_iota(jnp.int32, o.shape, 0) % MOBA_HEADS
        hcol = lax.broadcasted_iota(jnp.int32, o.shape, 1) // MOBA_DIM
        o = jnp.where(hrow == hcol, o, 0.0)
        for t in range(n_new):
            o_ref[0, t:t + 1, :] = jnp.sum(o[t * MOBA_HEADS:(t + 1) * MOBA_HEADS], axis=0,
                                           keepdims=True).astype(BF16)


def _moba_sample(page_table, qbd, qbdf, k_new, v_new, pool_kt, pool_vt, layer):
    nb, rows, _ = qbd.shape
    n_new = k_new.shape[1]
    n_pages = page_table.shape[1]
    pps = PAGES_PER_STEP
    past_len = n_pages * PAGE_SIZE
    pages_per_block = MOBA_BLOCK // PAGE_SIZE
    assert n_pages % pps == 0 and pps % pages_per_block == 0 and past_len % MOBA_BLOCK == 0 and n_new <= MOBA_BLOCK
    n_past_blocks = past_len // MOBA_BLOCK
    assert n_past_blocks <= LANE
    steps = n_pages // pps
    pt = page_table.reshape(-1)
    expand = (jnp.arange(past_len)[None, :] // MOBA_BLOCK == jnp.arange(LANE)[:, None]).astype(BF16)

    def kpage(i):
        return pl.BlockSpec((None, None, MOBA_WIDTH, PAGE_SIZE),
                            lambda b, ph, j, pt_ref: (layer, pt_ref[b * n_pages + (j * (1 - ph) + (steps - 1) * ph) * pps + i], 0, 0))

    def vpage(i):
        return pl.BlockSpec((None, None, MOBA_WIDTH, PAGE_SIZE),
                            lambda b, ph, j, pt_ref: (layer, pt_ref[b * n_pages + j * ph * pps + i], 0, 0))

    per_seq = lambda a: pl.BlockSpec((1,) + a.shape[1:], lambda b, ph, j, pt_ref: (b, 0, 0))
    grid_spec = pltpu.PrefetchScalarGridSpec(
        num_scalar_prefetch=1,
        grid=(nb, 2, steps),
        in_specs=[per_seq(qbd), per_seq(qbdf), per_seq(k_new), per_seq(v_new),
                  pl.BlockSpec(expand.shape, lambda b, ph, j, pt_ref: (0, 0))]
                 + [kpage(i) for i in range(pps)] + [vpage(i) for i in range(pps)],
        out_specs=pl.BlockSpec((1, n_new, MOBA_WIDTH), lambda b, ph, j, pt_ref: (b, 0, 0)),
        scratch_shapes=[pltpu.VMEM((rows, past_len), F32),
                        pltpu.VMEM((MOBA_WIDTH, LANE), F32),
                        pltpu.VMEM((rows, past_len), BF16),
                        pltpu.VMEM((rows, n_new), F32),
                        pltpu.VMEM((rows, 1), F32),
                        pltpu.VMEM((rows, MOBA_WIDTH), F32)],
    )
    return pl.pallas_call(
        functools.partial(_moba_sample_kernel, n_new=n_new, n_past_blocks=n_past_blocks),
        out_shape=jax.ShapeDtypeStruct((nb, n_new, MOBA_WIDTH), BF16),
        grid_spec=grid_spec,
        compiler_params=_cparams(("arbitrary", "arbitrary", "arbitrary")),
        name="moba_sample",
    )(pt, qbd, qbdf, k_new, v_new, expand, *([pool_kt] * pps), *([pool_vt] * pps))


def _s5_param_kernel(are_c_ref, aim_c_ref, dt_c_ref, bre_ref, bim_ref, are_r_ref, aim_r_ref, dt_r_ref,
                     bbre_ref, bbim_ref, pw_ref, step_ref):
    def abar(are, aim, dt):
        mag = jnp.exp(are * dt)
        return mag * jnp.cos(aim * dt), mag * jnp.sin(aim * dt)

    are, aim = are_c_ref[...], aim_c_ref[...]
    ar, ai = abar(are, aim, dt_c_ref[...])
    nr, ni = ar - 1.0, ai
    den = are * are + aim * aim
    cr = (nr * are + ni * aim) / den
    ci = (ni * are - nr * aim) / den
    bre, bim = bre_ref[...], bim_ref[...]
    bbre_ref[...] = cr * bre - ci * bim
    bbim_ref[...] = cr * bim + ci * bre
    ar, ai = abar(are_r_ref[...], aim_r_ref[...], dt_r_ref[...])
    pr, pi = ar, ai
    powers = []
    for i in range(SUBLANE):
        pw_ref[0, i:i + 1, :] = pr
        pw_ref[1, i:i + 1, :] = pi
        powers.append((pr, pi))
        pr, pi = pr * ar - pi * ai, pr * ai + pi * ar
    zero = jnp.zeros_like(ar)
    for k in range(3):
        d = 1 << k
        for i in range(SUBLANE):
            step_ref[2 * k, i:i + 1, :] = powers[d - 1][0] if i >= d else zero
            step_ref[2 * k + 1, i:i + 1, :] = powers[d - 1][1] if i >= d else zero


def _s5_params(a_re, a_im, log_dt, b_re, b_im, c_re, c_im):
    col = lambda a: a.reshape(SSM_CH, 1)
    rowv = lambda a: a.reshape(1, SSM_CH)
    dt = jnp.exp(log_dt.astype(F32))
    dt_full = jnp.broadcast_to(dt[:, None], (SSM_GROUPS, SSM_STATE))
    vm = lambda shape: pl.BlockSpec(shape, lambda: (0,) * len(shape))
    ins = [col(a_re), col(a_im), col(dt_full), b_re.reshape(SSM_CH, SSM_GROUP), b_im.reshape(SSM_CH, SSM_GROUP),
           rowv(a_re), rowv(a_im), rowv(dt_full)]
    bbre, bbim, pw, step = pl.pallas_call(
        _s5_param_kernel,
        out_shape=[jax.ShapeDtypeStruct((SSM_CH, SSM_GROUP), F32), jax.ShapeDtypeStruct((SSM_CH, SSM_GROUP), F32),
                   jax.ShapeDtypeStruct((2, SUBLANE, SSM_CH), F32), jax.ShapeDtypeStruct((6, SUBLANE, SSM_CH), F32)],
        in_specs=[vm(a.shape) for a in ins],
        out_specs=[vm((SSM_CH, SSM_GROUP)), vm((SSM_CH, SSM_GROUP)), vm((2, SUBLANE, SSM_CH)), vm((6, SUBLANE, SSM_CH))],
        compiler_params=pltpu.CompilerParams(vmem_limit_bytes=VMEM_LIMIT),
        name="s5_params",
    )(*ins)
    gpc = SSM_GROUPS // S5_CHUNKS
    eye = jnp.eye(gpc, dtype=F32)

    def in_mat(bb):
        b4 = bb.reshape(S5_CHUNKS, gpc, SSM_STATE, SSM_GROUP)
        return jnp.einsum("kgpc,gh->kgchp", b4, eye).reshape(S5_CHUNKS, S5_CIN, S5_CST)

    def out_mat(cc):
        c4 = cc.reshape(S5_CHUNKS, gpc, SSM_GROUP, SSM_STATE)
        return jnp.einsum("kgcp,gh->kgphc", c4, eye).reshape(S5_CHUNKS, S5_CST, S5_CIN)

    wb = jnp.concatenate([in_mat(bbre), in_mat(bbim)], axis=-1).astype(BF16)
    return dict(wb=wb, wc_re=out_mat(c_re.astype(F32)).astype(BF16), wc_im=out_mat(c_im.astype(F32)).astype(BF16),
                pw=pw, step=step)


def _s5_tail(x, y, d_ref, wglu_ref, g_ref, b_ref):
    z = _gelu(y + d_ref[...] * x)
    zz = _bdot(z, wglu_ref[...])
    out = zz[:, :D_MODEL] * _sigmoid(zz[:, D_MODEL:])
    return _layer_norm(ALPHA * x + out, g_ref[...], b_ref[...])


def _s5_prompt_kernel(x_ref, wb_ref, wcre_ref, wcim_ref, pw_ref, step_ref, d_ref, wglu_ref, g_ref, b_ref,
                      y_ref, hl_ref, carry_ref, h_scr):
    tt = x_ref.shape[0]

    @pl.when(pl.program_id(1) == 0)
    def _():
        carry_ref[...] = jnp.zeros(carry_ref.shape, F32)

    x = x_ref[...]
    ys = []
    for k in range(S5_CHUNKS):
        lo = k * S5_CST
        h_scr[...] = _bdot(x[:, k * S5_CIN:(k + 1) * S5_CIN], wb_ref[k])
        pr, pi = pw_ref[0, :, lo:lo + S5_CST], pw_ref[1, :, lo:lo + S5_CST]
        steps = [(step_ref[2 * s, :, lo:lo + S5_CST], step_ref[2 * s + 1, :, lo:lo + S5_CST], 1 << s)
                 for s in range(3)]

        def block(i, carry, pr=pr, pi=pi, steps=steps):
            cr, ci = carry
            r0 = pl.multiple_of(i * SUBLANE, SUBLANE)
            hr = h_scr[pl.ds(r0, SUBLANE), 0:S5_CST]
            hi = h_scr[pl.ds(r0, SUBLANE), S5_CST:2 * S5_CST]
            for sr, si, sh in steps:
                rr, ri = pltpu.roll(hr, sh, axis=0), pltpu.roll(hi, sh, axis=0)
                hr, hi = hr + sr * rr - si * ri, hi + sr * ri + si * rr
            hr, hi = hr + pr * cr - pi * ci, hi + pr * ci + pi * cr
            h_scr[pl.ds(r0, SUBLANE), 0:S5_CST] = hr
            h_scr[pl.ds(r0, SUBLANE), S5_CST:2 * S5_CST] = hi
            return hr[SUBLANE - 1:SUBLANE], hi[SUBLANE - 1:SUBLANE]

        cr, ci = lax.fori_loop(0, tt // SUBLANE, block,
                               (carry_ref[k, 0:1, 0:S5_CST], carry_ref[k, 0:1, S5_CST:2 * S5_CST]))
        carry_ref[k, 0:1, 0:S5_CST] = cr
        carry_ref[k, 0:1, S5_CST:2 * S5_CST] = ci
        ys.append(_bdot(h_scr[:, 0:S5_CST], wcre_ref[k]) - _bdot(h_scr[:, S5_CST:2 * S5_CST], wcim_ref[k]))
    y_ref[...] = _s5_tail(x, jnp.concatenate(ys, axis=-1), d_ref, wglu_ref, g_ref, b_ref)
    hl_ref[0] = carry_ref[...]


def _s5_prompt(x, sp, d, wglu, g, b, batch, seq):
    tt = ROW_TILE
    nt = seq // tt
    full = lambda a: pl.BlockSpec(a.shape, lambda bb, i: (0,) * a.ndim)
    y, hl = pl.pallas_call(
        _s5_prompt_kernel,
        out_shape=[jax.ShapeDtypeStruct((batch * seq, D_MODEL), F32),
                   jax.ShapeDtypeStruct((batch, S5_CHUNKS, SUBLANE, 2 * S5_CST), F32)],
        grid=(batch, nt),
        in_specs=[pl.BlockSpec((tt, D_MODEL), lambda bb, i: (bb * nt + i, 0)),
                  full(sp["wb"]), full(sp["wc_re"]), full(sp["wc_im"]), full(sp["pw"]), full(sp["step"]),
                  full(d), full(wglu), full(g), full(b)],
        out_specs=[pl.BlockSpec((tt, D_MODEL), lambda bb, i: (bb * nt + i, 0)),
                   pl.BlockSpec((1, S5_CHUNKS, SUBLANE, 2 * S5_CST), lambda bb, i: (bb, 0, 0, 0))],
        scratch_shapes=[pltpu.VMEM((S5_CHUNKS, SUBLANE, 2 * S5_CST), F32), pltpu.VMEM((tt, 2 * S5_CST), F32)],
        compiler_params=_cparams(("arbitrary", "arbitrary")),
        name="s5_prompt",
    )(x, sp["wb"], sp["wc_re"], sp["wc_im"], sp["pw"], sp["step"], d, wglu, g, b)
    h_re = hl[:, :, 0, :S5_CST].reshape(batch, SSM_GROUPS, SSM_STATE)
    h_im = hl[:, :, 0, S5_CST:].reshape(batch, SSM_GROUPS, SSM_STATE)
    return y, h_re, h_im


def _s5_sample_kernel(x_ref, h0re_ref, h0im_ref, wb_ref, wcre_ref, wcim_ref, pw_ref, d_ref, wglu_ref, g_ref, b_ref,
                      y_ref, hre_ref, him_ref):
    @pl.when(pl.program_id(0) == 0)
    def _():
        hre_ref[...] = h0re_ref[...]
        him_ref[...] = h0im_ref[...]

    x = x_ref[...]
    ys = []
    for k in range(S5_CHUNKS):
        lo = k * S5_CST
        bu = _bdot(x[:, k * S5_CIN:(k + 1) * S5_CIN], wb_ref[k])
        ar, ai = pw_ref[0, 0:1, lo:lo + S5_CST], pw_ref[1, 0:1, lo:lo + S5_CST]
        hr, hi = hre_ref[:, lo:lo + S5_CST], him_ref[:, lo:lo + S5_CST]
        nr = ar * hr - ai * hi + bu[:, :S5_CST]
        ni = ar * hi + ai * hr + bu[:, S5_CST:]
        hre_ref[:, lo:lo + S5_CST] = nr
        him_ref[:, lo:lo + S5_CST] = ni
        ys.append(_bdot(nr, wcre_ref[k]) - _bdot(ni, wcim_ref[k]))
    y_ref[...] = _s5_tail(x, jnp.concatenate(ys, axis=-1), d_ref, wglu_ref, g_ref, b_ref)


def _s5_sample(x, h0_re, h0_im, sp, d, wglu, g, b, nb, n_new):
    full = lambda a: pl.BlockSpec(a.shape, lambda t: (0,) * a.ndim)
    state = pl.BlockSpec((nb, SSM_CH), lambda t: (0, 0))
    y, h_re, h_im = pl.pallas_call(
        _s5_sample_kernel,
        out_shape=[jax.ShapeDtypeStruct((n_new * nb, D_MODEL), F32),
                   jax.ShapeDtypeStruct((nb, SSM_CH), F32), jax.ShapeDtypeStruct((nb, SSM_CH), F32)],
        grid=(n_new,),
        in_specs=[pl.BlockSpec((nb, D_MODEL), lambda t: (t, 0)), state, state,
                  full(sp["wb"]), full(sp["wc_re"]), full(sp["wc_im"]), full(sp["pw"]),
                  full(d), full(wglu), full(g), full(b)],
        out_specs=[pl.BlockSpec((nb, D_MODEL), lambda t: (t, 0)), state, state],
        compiler_params=_cparams(("arbitrary",)),
        name="s5_sample",
    )(x, h0_re.reshape(nb, SSM_CH), h0_im.reshape(nb, SSM_CH), sp["wb"], sp["wc_re"], sp["wc_im"], sp["pw"],
      d, wglu, g, b)
    return y, h_re.reshape(nb, SSM_GROUPS, SSM_STATE), h_im.reshape(nb, SSM_GROUPS, SSM_STATE)


def _ffn_prompt_kernel(x_ref, wup_ref, cw_ref, cb_ref, wdn_ref, g_ref, b_ref, y_ref, st_ref,
                       carry_ref, ua_ref, ub_ref, act_ref):
    tm = x_ref.shape[0]

    @pl.when(pl.program_id(1) == 0)
    def _():
        carry_ref[...] = jnp.zeros(carry_ref.shape, F32)

    x = x_ref[...]
    xb = x.astype(BF16)

    def conv(col0, scr):
        cols = slice(col0, col0 + FFN_CW)
        u = jnp.dot(xb, wup_ref[:, cols], preferred_element_type=F32)
        scr[0:SUBLANE, :] = carry_ref[:, cols]
        scr[SUBLANE:SUBLANE + tm, :] = u
        carry_ref[:, cols] = u[tm - SUBLANE:tm]
        p1 = scr[SUBLANE - 1:SUBLANE - 1 + tm, :]
        p2 = scr[SUBLANE - 2:SUBLANE - 2 + tm, :]
        return cb_ref[:, cols] + cw_ref[0:1, cols] * p2 + cw_ref[1:2, cols] * p1 + cw_ref[2:3, cols] * u

    for c in range(D_FF // FFN_CW):
        act = _gelu(conv(c * FFN_CW, ua_ref)) * conv(D_FF + c * FFN_CW, ub_ref)
        act_ref[:, c * FFN_CW:(c + 1) * FFN_CW] = act.astype(BF16)
    down = jnp.dot(act_ref[...], wdn_ref[...], preferred_element_type=F32)
    y_ref[...] = _layer_norm(ALPHA * x + down, g_ref[...], b_ref[...])
    st_ref[0] = carry_ref[...]


def _ffn_prompt(x, w, g, b, batch, seq):
    tm = ROW_TILE
    nt = seq // tm
    assert CONV_W - 1 <= SUBLANE and D_FF % FFN_CW == 0
    full = lambda a: pl.BlockSpec(a.shape, lambda bb, i: (0,) * a.ndim)
    y, st = pl.pallas_call(
        _ffn_prompt_kernel,
        out_shape=[jax.ShapeDtypeStruct((batch * seq, D_MODEL), F32),
                   jax.ShapeDtypeStruct((batch, SUBLANE, 2 * D_FF), F32)],
        grid=(batch, nt),
        in_specs=[pl.BlockSpec((tm, D_MODEL), lambda bb, i: (bb * nt + i, 0)),
                  full(w["w_up"]), full(w["conv_w"]), full(w["conv_b"]), full(w["w_down"]), full(g), full(b)],
        out_specs=[pl.BlockSpec((tm, D_MODEL), lambda bb, i: (bb * nt + i, 0)),
                   pl.BlockSpec((1, SUBLANE, 2 * D_FF), lambda bb, i: (bb, 0, 0))],
        scratch_shapes=[pltpu.VMEM((SUBLANE, 2 * D_FF), F32),
                        pltpu.VMEM((SUBLANE + tm, FFN_CW), F32), pltpu.VMEM((SUBLANE + tm, FFN_CW), F32),
                        pltpu.VMEM((tm, D_FF), BF16)],
        compiler_params=_cparams(("arbitrary", "arbitrary")),
        name="ffn_prompt",
    )(x, w["w_up"], w["conv_w"], w["conv_b"], w["w_down"], g, b)
    return y, st[:, SUBLANE - (CONV_W - 1):, :]


def _ffn_sample_kernel(x_ref, st_ref, wup_ref, cw_ref, cb_ref, wdn_ref, g_ref, b_ref, y_ref, ns_ref, act_ref):
    tm = x_ref.shape[0]
    w2 = 2 * D_FF

    @pl.when(pl.program_id(0) == 0)
    def _():
        ns_ref[...] = st_ref[...]

    x = x_ref[...]
    xb = x.astype(BF16)

    def conv(col0):
        cols = slice(col0, col0 + FFN_CW)
        cols1 = slice(w2 + col0, w2 + col0 + FFN_CW)
        u = jnp.dot(xb, wup_ref[:, cols], preferred_element_type=F32)
        p2 = ns_ref[:, cols]
        p1 = ns_ref[:, cols1]
        ns_ref[:, cols] = p1
        ns_ref[:, cols1] = u
        return cb_ref[:, cols] + cw_ref[0:1, cols] * p2 + cw_ref[1:2, cols] * p1 + cw_ref[2:3, cols] * u

    for c in range(D_FF // FFN_CW):
        act = _gelu(conv(c * FFN_CW)) * conv(D_FF + c * FFN_CW)
        act_ref[:, c * FFN_CW:(c + 1) * FFN_CW] = act.astype(BF16)
    down = jnp.dot(act_ref[...], wdn_ref[...], preferred_element_type=F32)
    y_ref[...] = _layer_norm(ALPHA * x + down, g_ref[...], b_ref[...])


def _ffn_sample(x, state, w, g, b, nb, n_new):
    assert CONV_W == 3
    full = lambda a: pl.BlockSpec(a.shape, lambda t: (0,) * a.ndim)
    st = state.reshape(nb, (CONV_W - 1) * 2 * D_FF)
    y, ns = pl.pallas_call(
        _ffn_sample_kernel,
        out_shape=[jax.ShapeDtypeStruct((n_new * nb, D_MODEL), F32), jax.ShapeDtypeStruct(st.shape, F32)],
        grid=(n_new,),
        in_specs=[pl.BlockSpec((nb, D_MODEL), lambda t: (t, 0)), full(st),
                  full(w["w_up"]), full(w["conv_w"]), full(w["conv_b"]), full(w["w_down"]), full(g), full(b)],
        out_specs=[pl.BlockSpec((nb, D_MODEL), lambda t: (t, 0)), full(st)],
        scratch_shapes=[pltpu.VMEM((nb, D_FF), BF16)],
        compiler_params=_cparams(("arbitrary",)),
        name="ffn_sample",
    )(x, st, w["w_up"], w["conv_w"], w["conv_b"], w["w_down"], g, b)
    return y, ns.reshape(nb, CONV_W - 1, 2 * D_FF)


def _attn_weights(w_in, q_norm, w_q_b, kv_norm, w_kv_b, w_out):
    o1 = Q_RANK
    o2 = o1 + KV_RANK
    o3 = o2 + ROPE_DIM
    kpe = w_in[:, o2:o3]
    w_in_r = jnp.concatenate([w_in[:, :o2], w_in[:, o3:]] + [kpe] * (LANE // ROPE_DIM), axis=1).astype(BF16)
    assert w_in_r.shape[1] == W_IN_COLS
    qb = w_q_b.reshape(Q_RANK, MLA_HEADS, NOPE_DIM + ROPE_DIM)
    nope = jnp.pad(qb[:, :, :NOPE_DIM], ((0, 0), (0, 0), (0, LANE - NOPE_DIM))).reshape(Q_RANK, MLA_HEADS * LANE)
    ropew = qb[:, :, NOPE_DIM:].reshape(Q_RANK, MLA_HEADS * ROPE_DIM)
    w_q_b_r = jnp.concatenate([nope, ropew], axis=1).astype(BF16)
    wk = jnp.transpose(w_kv_b[:, :, :NOPE_DIM], (1, 2, 0))
    wk = jnp.pad(wk, ((0, 0), (0, LANE - NOPE_DIM), (0, 0))).astype(BF16)
    wv = jnp.transpose(w_kv_b[:, :, NOPE_DIM:], (1, 0, 2))
    wv_lo = jnp.pad(wv, ((0, 0), (0, 0), (0, LANE - V_DIM)))
    wv_hi = jnp.pad(wv, ((0, 0), (0, 0), (LANE - V_DIM, 0)))
    odd = (jnp.arange(MLA_HEADS) % 2 == 1)[:, None, None]
    wvp = jnp.where(odd, wv_hi, wv_lo).astype(BF16)
    return dict(w_in=w_in_r, q_norm=q_norm.reshape(1, Q_RANK), w_q_b=w_q_b_r, wk=wk,
                kv_norm=kv_norm.reshape(1, KV_RANK), wv=wvp, w_out=w_out.astype(BF16))


def _ffn_weights(w_up, conv_w, conv_b, w_down):
    return dict(w_up=w_up.astype(BF16), conv_w=conv_w, conv_b=conv_b.reshape(1, 2 * D_FF), w_down=w_down.astype(BF16))


def _to_time_major(a):
    return jnp.swapaxes(a, 0, 1).reshape((a.shape[0] * a.shape[1],) + a.shape[2:])


def _from_time_major(a, nb):
    return jnp.swapaxes(a.reshape((a.shape[0] // nb, nb) + a.shape[1:]), 0, 1)


def kernel(x_prompt, x_sample, cache_mla_latent, cache_mla_rope, cache_moba_k, cache_moba_v, state_s5_re, state_s5_im, state_ffn_conv, page_table, ln_mix_g, ln_mix_b, ln_ffn_g, ln_ffn_b, att_w_in, mla_q_norm, mla_w_q_b, mla_kv_norm, mla_w_kv_b, att_w_out, s5_a_re, s5_a_im, s5_log_dt, s5_b_re, s5_b_im, s5_c_re, s5_c_im, s5_d, s5_w_glu, ffn_w_up, ffn_conv_w, ffn_conv_b, ffn_w_down):
    batch, seq, _ = x_prompt.shape
    nb, n_new, _ = x_sample.shape
    past_len = page_table.shape[1] * PAGE_SIZE
    depth = ln_mix_g.shape[0]
    assert seq % MLA_TK == 0 and nb % SUBLANE == 0 and (nb * n_new) % ROW_TILE == 0

    xp = x_prompt.reshape(batch * seq, D_MODEL)
    xs = _to_time_major(x_sample)
    tabs_p = _rope_tables(seq, 0)
    tabs_s = [jnp.repeat(t[:n_new], nb, axis=0) for t in _rope_tables(SUBLANE * pl.cdiv(n_new, SUBLANE), past_len)]
    row = lambda v: v.reshape(1, -1)
    pool_rope_t = jnp.swapaxes(cache_mla_rope, 2, 3)

    def tokens_minor(pool):
        return jnp.transpose(pool, (0, 1, 3, 4, 2)).reshape(pool.shape[0], pool.shape[1], MOBA_WIDTH, PAGE_SIZE)

    pool_kt, pool_vt = tokens_minor(cache_moba_k), tokens_minor(cache_moba_v)

    outs_p ={k: [] for k in ("lat", "rope", "mk", "mv", "sre", "sim", "conv")}
    outs_s = {k: [] for k in ("lat", "rope", "mk", "mv", "sre", "sim", "conv")}
    for l in range(depth):
        g_mix, b_mix = row(ln_mix_g[l]), row(ln_mix_b[l])
        if l % 2 == 0:
            a = l // 2
            w = _attn_weights(att_w_in[a], mla_q_norm[a], mla_w_q_b[a], mla_kv_norm[a], mla_w_kv_b[a], att_w_out[a])
            qcat, kcat, lat, rp, qm, km, kaug, vm, vmb, kmean = _attn_proj(xp, w, tabs_p, seq)
            o_mla = _mla_prompt(qcat, kcat, batch, seq)
            qaug = _moba_select(qm, kmean, batch, seq)
            o_moba = _moba_prompt(qaug, kaug, vmb, batch, seq)
            xp = _attn_out(xp, o_mla, o_moba, w, g_mix, b_mix)
            outs_p["lat"].append(lat.reshape(batch, seq, KV_RANK))
            outs_p["rope"].append(rp.reshape(batch, seq, ROPE_DIM))
            outs_p["mk"].append(km.reshape(batch, seq, MOBA_HEADS, MOBA_DIM))
            outs_p["mv"].append(vm.reshape(batch, seq, MOBA_HEADS, MOBA_DIM))
            qcat, _, lat, rp, qm, km, _, vm, _, _ = _attn_proj(xs, w, tabs_s, nb * n_new)
            lat_b, rp_b = _from_time_major(lat, nb), _from_time_major(rp, nb)
            km_b, vm_b = _from_time_major(km, nb), _from_time_major(vm, nb)
            qm_b = _from_time_major(qm, nb)
            qc = qcat.reshape(n_new, nb // LANE, MLA_HEADS, LANE, 2 * LANE)
            qc = jnp.transpose(qc, (1, 3, 0, 2, 4)).reshape(nb, n_new * MLA_HEADS, 2 * LANE)
            qlat = qc[:, :, :KV_RANK]
            qr4 = qc[:, :, KV_RANK:].reshape(nb, n_new, MLA_HEADS, LANE // ROPE_DIM, ROPE_DIM)
            grp = (jnp.arange(MLA_HEADS) % (LANE // ROPE_DIM))
            qrope = jnp.take_along_axis(qr4, grp[None, None, :, None, None], axis=3)[:, :, :, 0, :]
            qrope = qrope.reshape(nb, n_new * MLA_HEADS, ROPE_DIM)
            o_mla = _mla_sample(page_table, qlat, qrope, lat_b, rp_b.astype(BF16), cache_mla_latent, pool_rope_t, a)
            q4 = qm_b.reshape(nb, n_new, MOBA_HEADS, MOBA_DIM)
            eye = jnp.eye(MOBA_HEADS, dtype=F32)
            qbdf = jnp.einsum("bthd,hk->bthkd", q4, eye).reshape(nb, n_new * MOBA_HEADS, MOBA_WIDTH)
            o_moba = _moba_sample(page_table, qbdf.astype(BF16), qbdf, km_b.astype(BF16), vm_b.astype(BF16),
                                  pool_kt, pool_vt, a)
            om = o_mla.reshape(nb // LANE, LANE, n_new, MLA_HEADS, KV_RANK)
            om = jnp.transpose(om, (2, 0, 3, 1, 4)).reshape(n_new * nb // LANE, MLA_HEADS, LANE, KV_RANK)
            xs = _attn_out(xs, om, _to_time_major(o_moba), w, g_mix, b_mix)
            outs_s["lat"].append(lat_b)
            outs_s["rope"].append(rp_b)
            outs_s["mk"].append(km_b.reshape(nb, n_new, MOBA_HEADS, MOBA_DIM))
            outs_s["mv"].append(vm_b.reshape(nb, n_new, MOBA_HEADS, MOBA_DIM))
        else:
            s = l // 2
            sp = _s5_params(s5_a_re[s], s5_a_im[s], s5_log_dt[s], s5_b_re[s], s5_b_im[s], s5_c_re[s], s5_c_im[s])
            d, wglu = row(s5_d[s]), s5_w_glu[s].astype(BF16)
            xp, hre, him = _s5_prompt(xp, sp, d, wglu, g_mix, b_mix, batch, seq)
            outs_p["sre"].append(hre)
            outs_p["sim"].append(him)
            xs, hre, him = _s5_sample(xs, state_s5_re[s], state_s5_im[s], sp, d, wglu, g_mix, b_mix, nb, n_new)
            outs_s["sre"].append(hre)
            outs_s["sim"].append(him)
        fw = _ffn_weights(ffn_w_up[l], ffn_conv_w[l], ffn_conv_b[l], ffn_w_down[l])
        g_ffn, b_ffn = row(ln_ffn_g[l]), row(ln_ffn_b[l])
        xp, cp = _ffn_prompt(xp, fw, g_ffn, b_ffn, batch, seq)
        xs, cs = _ffn_sample(xs, state_ffn_conv[l], fw, g_ffn, b_ffn, nb, n_new)
        outs_p["conv"].append(cp)
        outs_s["conv"].append(cs)

    keys = ("lat", "rope", "mk", "mv", "sre", "sim", "conv")
    return ((xp.reshape(batch, seq, D_MODEL), _from_time_major(xs, nb))
            + tuple(jnp.stack(outs_p[k]) for k in keys) + tuple(jnp.stack(outs_s[k]) for k in keys))
```

```python
import functools
import math

import jax
import jax.numpy as jnp
from jax import lax
from jax.experimental import pallas as pl
from jax.experimental.pallas import tpu as pltpu

F32 = jnp.float32
BF16 = jnp.bfloat16

D_MODEL = 1024
PAGE_SIZE = 128
MLA_HEADS = 8
Q_RANK = 256
KV_RANK = 128
NOPE_DIM = 64
ROPE_DIM = 32
V_DIM = 64
MLA_SCALE = (NOPE_DIM + ROPE_DIM) ** -0.5
MOBA_HEADS = 8
MOBA_DIM = 64
MOBA_WIDTH = MOBA_HEADS * MOBA_DIM
MOBA_BLOCK = 256
MOBA_TOPK = 3
MOBA_SCALE = MOBA_DIM ** -0.5
ROPE_THETA = 10000.0
SSM_GROUP = 16
SSM_GROUPS = D_MODEL // SSM_GROUP
SSM_STATE = 64
SSM_CH = SSM_GROUPS * SSM_STATE
D_FF = 2816
CONV_W = 3
DEPTH = 4
ALPHA = (2 * DEPTH) ** 0.25
LN_EPS = 1e-5
RMS_EPS = 1e-6

LANE = 128
SUBLANE = 8
NEG = -1e30
LOG2E = math.log2(math.e)
ROW_TILE = 256
MLA_TQ = 128
MLA_TK = 1024
MOBA_GROUP = 4
S5_CHUNKS = 4
S5_CIN = D_MODEL // S5_CHUNKS
S5_CST = SSM_CH // S5_CHUNKS
FFN_CW = 256
PAGES_PER_STEP = 16
W_IN_COLS = Q_RANK + KV_RANK + 3 * MOBA_WIDTH + LANE
VMEM_LIMIT = 56 * 1024 * 1024


def _cparams(sem):
    return pltpu.CompilerParams(dimension_semantics=sem, vmem_limit_bytes=VMEM_LIMIT)


def _bdot(a, b):
    return jnp.dot(a.astype(BF16), b.astype(BF16), preferred_element_type=F32)


def _bdot_nt(a, b):
    return lax.dot_general(a.astype(BF16), b.astype(BF16), (((1,), (1,)), ((), ())),
                           preferred_element_type=F32)


def _layer_norm(v, g, b):
    mu = jnp.mean(v, axis=-1, keepdims=True)
    d = v - mu
    var = jnp.mean(d * d, axis=-1, keepdims=True)
    return d * lax.rsqrt(var + LN_EPS) * g + b


def _rms_norm(v, g):
    return v * lax.rsqrt(jnp.mean(v * v, axis=-1, keepdims=True) + RMS_EPS) * g


def _gelu(v):
    return 0.5 * v * (1.0 + jnp.tanh(math.sqrt(2.0 / math.pi) * (v + 0.044715 * (v * v * v))))


def _sigmoid(v):
    return 1.0 / (1.0 + jnp.exp(-v))


def _swap_halves(v, half):
    w = v.shape[-1]
    lane = lax.broadcasted_iota(jnp.int32, v.shape, 1)
    fwd = pltpu.roll(v, w - half, axis=1)
    bwd = pltpu.roll(v, half, axis=1)
    return jnp.where((lane % (2 * half)) < half, fwd, bwd)


def _rope(v, cos, sin_signed, half):
    outs = []
    for c in range(v.shape[-1] // LANE):
        vc = v[:, c * LANE:(c + 1) * LANE]
        outs.append(vc * cos + _swap_halves(vc, half) * sin_signed)
    return outs[0] if len(outs) == 1 else jnp.concatenate(outs, axis=-1)


def _rope_table_kernel(inv64_ref, sg64_ref, inv32_ref, sg32_ref, c64_ref, s64_ref, c32_ref, s32_ref, *, offset):
    rows = c64_ref.shape[0]
    base = pl.program_id(0) * rows + offset
    pos = (base + lax.broadcasted_iota(jnp.int32, (rows, LANE), 0)).astype(F32)
    a64 = pos * inv64_ref[...]
    c64_ref[...] = jnp.cos(a64)
    s64_ref[...] = jnp.sin(a64) * sg64_ref[...]
    a32 = pos * inv32_ref[...]
    c32_ref[...] = jnp.cos(a32)
    s32_ref[...] = jnp.sin(a32) * sg32_ref[...]


def _rope_tables(n_rows, offset):
    def lane_consts(d):
        half = d // 2
        inv = ROPE_THETA ** (-jnp.arange(half, dtype=F32) * 2.0 / d)
        l = jnp.arange(LANE)
        return (inv[(l % d) % half].reshape(1, LANE),
                jnp.where((l % d) < half, -1.0, 1.0).astype(F32).reshape(1, LANE))
    inv64, sg64 = lane_consts(MOBA_DIM)
    inv32, sg32 = lane_consts(ROPE_DIM)
    rows = min(n_rows, 512)
    assert n_rows % rows == 0 and rows % SUBLANE == 0
    const = pl.BlockSpec((1, LANE), lambda i: (0, 0))
    out = pl.BlockSpec((rows, LANE), lambda i: (i, 0))
    return pl.pallas_call(
        functools.partial(_rope_table_kernel, offset=offset),
        out_shape=[jax.ShapeDtypeStruct((n_rows, LANE), F32)] * 4,
        grid=(n_rows // rows,),
        in_specs=[const] * 4,
        out_specs=[out] * 4,
        compiler_params=_cparams(("arbitrary",)),
        name="rope_tables",
    )(inv64, sg64, inv32, sg32)


def _proj_kernel(x_ref, win_ref, qn_ref, wqb_ref, wk_ref, kvn_ref, c64_ref, s64_ref, c32_ref, s32_ref,
                 qcat_ref, kcat_ref, lat_ref, rope_ref, qm_ref, km_ref, kaug_ref, vm_ref, vmb_ref, kmean_ref,
                 *, blocks_per_seq):
    tm = x_ref.shape[0]
    nsub = tm // LANE
    c64, s64, c32, s32 = c64_ref[...], s64_ref[...], c32_ref[...], s32_ref[...]
    h = _bdot(x_ref[...], win_ref[...])
    o_kv = Q_RANK
    o_qm = o_kv + KV_RANK
    o_km = o_qm + MOBA_WIDTH
    o_vm = o_km + MOBA_WIDTH
    o_pe = o_vm + MOBA_WIDTH

    qn = _rms_norm(h[:, :Q_RANK], qn_ref[...])
    q = _bdot(qn, wqb_ref[...])
    o_qr = MLA_HEADS * LANE
    qr = _rope(q[:, o_qr:o_qr + MLA_HEADS * ROPE_DIM], c32, s32, ROPE_DIM // 2)
    lane = lax.broadcasted_iota(jnp.int32, (tm, LANE), 1)
    heads_per_chunk = LANE // ROPE_DIM
    for hd in range(MLA_HEADS):
        ql = _bdot(q[:, hd * LANE:(hd + 1) * LANE], wk_ref[hd]).astype(BF16)
        grp = hd % heads_per_chunk
        chunk = qr[:, (hd // heads_per_chunk) * LANE:(hd // heads_per_chunk + 1) * LANE]
        qrh = jnp.where((lane >= grp * ROPE_DIM) & (lane < (grp + 1) * ROPE_DIM), chunk, 0.0).astype(BF16)
        for sb in range(nsub):
            qcat_ref[sb, hd, :, 0:LANE] = ql[sb * LANE:(sb + 1) * LANE]
            qcat_ref[sb, hd, :, LANE:2 * LANE] = qrh[sb * LANE:(sb + 1) * LANE]

    lat = _rms_norm(h[:, o_kv:o_kv + KV_RANK], kvn_ref[...])
    lat_ref[...] = lat
    kpe = _rope(h[:, o_pe:o_pe + LANE], c32, s32, ROPE_DIM // 2)
    rope_ref[...] = kpe[:, :ROPE_DIM]
    kcat_ref[:, 0:KV_RANK] = lat.astype(BF16)
    kcat_ref[:, KV_RANK:KV_RANK + LANE] = kpe.astype(BF16)

    qm_ref[...] = _rope(h[:, o_qm:o_qm + MOBA_WIDTH], c64, s64, MOBA_DIM // 2)
    km = _rope(h[:, o_km:o_km + MOBA_WIDTH], c64, s64, MOBA_DIM // 2)
    km_ref[...] = km
    kmean_ref[0] = jnp.mean(km, axis=0, keepdims=True)
    vm = h[:, o_vm:o_vm + MOBA_WIDTH]
    vm_ref[...] = vm
    vmb_ref[...] = vm.astype(BF16)
    blk = pl.program_id(0) % blocks_per_seq
    onehot = jnp.where(lane == MOBA_DIM + blk, 1.0, 0.0)
    for g in range(MOBA_HEADS // 2):
        ch = km[:, g * LANE:(g + 1) * LANE]
        kaug_ref[:, (2 * g) * LANE:(2 * g + 1) * LANE] = jnp.where(lane < MOBA_DIM, ch, onehot).astype(BF16)
        chs = pltpu.roll(ch, MOBA_DIM, axis=1)
        kaug_ref[:, (2 * g + 1) * LANE:(2 * g + 2) * LANE] = jnp.where(lane < MOBA_DIM, chs, onehot).astype(BF16)


def _attn_proj(x, w, tabs, seq_rows):
    n = x.shape[0]
    tm = ROW_TILE
    assert tm == MOBA_BLOCK and n % tm == 0 and seq_rows % tm == 0
    tiles_per_seq = seq_rows // tm
    nsub = tm // LANE
    c64, s64, c32, s32 = tabs
    row = lambda width: pl.BlockSpec((tm, width), lambda i: (i, 0))
    tab = pl.BlockSpec((tm, LANE), lambda i: (i % tiles_per_seq, 0))
    full = lambda a: pl.BlockSpec(a.shape, lambda i: (0,) * a.ndim)
    outs = pl.pallas_call(
        functools.partial(_proj_kernel, blocks_per_seq=tiles_per_seq),
        out_shape=[
            jax.ShapeDtypeStruct((n // LANE, MLA_HEADS, LANE, 2 * LANE), BF16),
            jax.ShapeDtypeStruct((n, 2 * LANE), BF16),
            jax.ShapeDtypeStruct((n, KV_RANK), F32),
            jax.ShapeDtypeStruct((n, ROPE_DIM), F32),
            jax.ShapeDtypeStruct((n, MOBA_WIDTH), F32),
            jax.ShapeDtypeStruct((n, MOBA_WIDTH), F32),
            jax.ShapeDtypeStruct((n, MOBA_HEADS * LANE), BF16),
            jax.ShapeDtypeStruct((n, MOBA_WIDTH), F32),
            jax.ShapeDtypeStruct((n, MOBA_WIDTH), BF16),
            jax.ShapeDtypeStruct((n // tm, 1, MOBA_WIDTH), F32),
        ],
        grid=(n // tm,),
        in_specs=[row(D_MODEL), full(w["w_in"]), full(w["q_norm"]), full(w["w_q_b"]), full(w["wk"]),
                  full(w["kv_norm"]), tab, tab, tab, tab],
        out_specs=[
            pl.BlockSpec((nsub, MLA_HEADS, LANE, 2 * LANE), lambda i: (i, 0, 0, 0)),
            row(2 * LANE), row(KV_RANK), row(ROPE_DIM), row(MOBA_WIDTH), row(MOBA_WIDTH),
            row(MOBA_HEADS * LANE), row(MOBA_WIDTH), row(MOBA_WIDTH),
            pl.BlockSpec((1, 1, MOBA_WIDTH), lambda i: (i, 0, 0)),
        ],
        compiler_params=_cparams(("arbitrary",)),
        name="attn_proj",
    )(x, w["w_in"], w["q_norm"], w["w_q_b"], w["wk"], w["kv_norm"], c64, s64, c32, s32)
    return outs


def _mla_prompt_kernel(q_ref, kcat_ref, o_ref, m_ref, l_ref, acc_ref):
    qi = pl.program_id(1)
    rows = MLA_HEADS * MLA_TQ
    q = q_ref[0].reshape(rows, 2 * LANE)
    cexp = MLA_SCALE * LOG2E

    def scores(kt):
        k = kcat_ref[pl.ds(pl.multiple_of(kt * MLA_TK, MLA_TK), MLA_TK), :]
        return _bdot_nt(q, k), k[:, :KV_RANK]

    kd = (qi * MLA_TQ) // MLA_TK
    s, v = scores(kd)
    col = kd * MLA_TK + lax.broadcasted_iota(jnp.int32, (rows, MLA_TK), 1)
    rowpos = qi * MLA_TQ + lax.broadcasted_iota(jnp.int32, (rows, MLA_TK), 0) % MLA_TQ
    s = jnp.where(col <= rowpos, s, NEG)
    m = jnp.max(s, axis=1, keepdims=True)
    p = jnp.exp2((s - m) * cexp)
    m_ref[...] = m
    l_ref[...] = jnp.sum(p, axis=1, keepdims=True)
    acc_ref[...] = jnp.dot(p.astype(BF16), v, preferred_element_type=F32)

    def body(kt, carry):
        s, v = scores(kt)
        m_old = m_ref[...]
        m_new = jnp.maximum(m_old, jnp.max(s, axis=1, keepdims=True))
        alpha = jnp.exp2((m_old - m_new) * cexp)
        p = jnp.exp2((s - m_new) * cexp)
        m_ref[...] = m_new
        l_ref[...] = l_ref[...] * alpha + jnp.sum(p, axis=1, keepdims=True)
        acc_ref[...] = acc_ref[...] * alpha + jnp.dot(p.astype(BF16), v, preferred_element_type=F32)
        return carry

    lax.fori_loop(0, kd, body, 0)
    o = (acc_ref[...] / l_ref[...]).astype(BF16)
    o_ref[0] = o.reshape(MLA_HEADS, MLA_TQ, KV_RANK)


def _mla_prompt(qcat, kcat, batch, seq):
    assert MLA_TQ == LANE and seq % MLA_TK == 0 and MLA_TK % MLA_TQ == 0
    nq = seq // MLA_TQ
    rows = MLA_HEADS * MLA_TQ
    return pl.pallas_call(
        _mla_prompt_kernel,
        out_shape=jax.ShapeDtypeStruct((batch * nq, MLA_HEADS, MLA_TQ, KV_RANK), BF16),
        grid=(batch, nq),
        in_specs=[pl.BlockSpec((1, MLA_HEADS, MLA_TQ, 2 * LANE), lambda b, i: (b * nq + i, 0, 0, 0)),
                  pl.BlockSpec((seq, 2 * LANE), lambda b, i: (b, 0))],
        out_specs=pl.BlockSpec((1, MLA_HEADS, MLA_TQ, KV_RANK), lambda b, i: (b * nq + i, 0, 0, 0)),
        scratch_shapes=[pltpu.VMEM((rows, 1), F32), pltpu.VMEM((rows, 1), F32), pltpu.VMEM((rows, KV_RANK), F32)],
        compiler_params=_cparams(("arbitrary", "arbitrary")),
        name="mla_prompt",
    )(qcat, kcat)


def _class_reduce(v, op):
    for sh in (8, 16, 32, 64):
        v = op(v, pltpu.roll(v, sh, axis=1))
    return v


def _moba_select_kernel(qm_ref, kmbd_ref, perm_ref, qaug_ref, *, blocks_per_seq):
    tm = qm_ref.shape[0]
    qm = qm_ref[...]
    gate = jnp.dot(qm, kmbd_ref[...], precision=lax.Precision.HIGHEST, preferred_element_type=F32)
    lane = lax.broadcasted_iota(jnp.int32, (tm, LANE), 1)
    nblk = lane // MOBA_HEADS
    own = pl.program_id(0) % blocks_per_seq
    past = nblk < own
    g = jnp.where(past, gate, -jnp.inf)
    picked = jnp.zeros((tm, LANE), F32)
    for _ in range(MOBA_TOPK):
        top = _class_reduce(g, jnp.maximum)
        first = _class_reduce(jnp.where(g == top, nblk, LANE), jnp.minimum)
        hit = nblk == first
        picked = picked + jnp.where(hit & (top > -jnp.inf), 1.0, 0.0)
        g = jnp.where(hit, -jnp.inf, g)
    masked = jnp.where(past & (picked > 0.5), 0.0, 1.0)
    bias = jnp.dot(masked.astype(BF16), perm_ref[...], preferred_element_type=F32) * NEG
    for gp in range(MOBA_HEADS // 2):
        ch = qm[:, gp * LANE:(gp + 1) * LANE]
        for e, src in ((0, ch), (1, pltpu.roll(ch, MOBA_DIM, axis=1))):
            hd = 2 * gp + e
            slot = jnp.where(lane < MOBA_DIM, src, 0.0) + bias[:, hd * LANE:(hd + 1) * LANE]
            qaug_ref[:, hd * LANE:(hd + 1) * LANE] = slot.astype(BF16)


def _moba_select(qm, kmean, batch, seq):
    n = qm.shape[0]
    tm = ROW_TILE
    nblk = seq // MOBA_BLOCK
    assert tm == MOBA_BLOCK and nblk * MOBA_HEADS <= LANE
    km4 = kmean.reshape(batch, nblk, MOBA_HEADS, MOBA_DIM)
    eye = jnp.eye(MOBA_HEADS, dtype=F32)
    kmbd = jnp.einsum("bnhd,hk->bhdnk", km4, eye).reshape(batch, MOBA_WIDTH, nblk * MOBA_HEADS)
    kmbd = jnp.pad(kmbd, ((0, 0), (0, 0), (0, LANE - nblk * MOBA_HEADS)))
    src = jnp.arange(LANE)
    dst = (src % MOBA_HEADS) * LANE + MOBA_DIM + src // MOBA_HEADS
    perm = (jnp.arange(MOBA_HEADS * LANE)[None, :] == dst[:, None]).astype(BF16)
    return pl.pallas_call(
        functools.partial(_moba_select_kernel, blocks_per_seq=nblk),
        out_shape=jax.ShapeDtypeStruct((n, MOBA_HEADS * LANE), BF16),
        grid=(n // tm,),
        in_specs=[pl.BlockSpec((tm, MOBA_WIDTH), lambda i: (i, 0)),
                  pl.BlockSpec((None, MOBA_WIDTH, LANE), lambda i: (i // nblk, 0, 0)),
                  pl.BlockSpec(perm.shape, lambda i: (0, 0))],
        out_specs=pl.BlockSpec((tm, MOBA_HEADS * LANE), lambda i: (i, 0)),
        compiler_params=_cparams(("arbitrary",)),
        name="moba_select",
    )(qm, kmbd, perm)


def _moba_prompt_kernel(qaug_ref, kaug_ref, v_ref, o_ref, m_ref, l_ref, acc_ref):
    own = pl.program_id(1)
    tq = qaug_ref.shape[0]
    cexp = MOBA_SCALE * LOG2E
    lane = lax.broadcasted_iota(jnp.int32, (tq, LANE), 1)

    def scores(r0, rows, hd, with_bias):
        q = qaug_ref[:, hd * LANE:(hd + 1) * LANE]
        if not with_bias:
            q = jnp.where(lane < MOBA_DIM, q.astype(F32), 0.0).astype(BF16)
        k = kaug_ref[pl.ds(r0, rows), hd * LANE:(hd + 1) * LANE]
        v = v_ref[pl.ds(r0, rows), (hd // 2) * LANE:(hd // 2 + 1) * LANE]
        return _bdot_nt(q, k), v

    tri = (lax.broadcasted_iota(jnp.int32, (tq, MOBA_BLOCK), 1)
           <= lax.broadcasted_iota(jnp.int32, (tq, MOBA_BLOCK), 0))
    for hd in range(MOBA_HEADS):
        s, v = scores(pl.multiple_of(own * MOBA_BLOCK, MOBA_BLOCK), MOBA_BLOCK, hd, False)
        s = jnp.where(tri, s, NEG)
        m = jnp.max(s, axis=1, keepdims=True)
        p = jnp.exp2((s - m) * cexp)
        m_ref[hd] = m
        l_ref[hd] = jnp.sum(p, axis=1, keepdims=True)
        acc_ref[hd] = jnp.dot(p.astype(BF16), v, preferred_element_type=F32)

    group_rows = MOBA_GROUP * MOBA_BLOCK

    def body(gi, carry):
        for hd in range(MOBA_HEADS):
            s, v = scores(pl.multiple_of(gi * group_rows, group_rows), group_rows, hd, True)
            m_old = m_ref[hd]
            m_new = jnp.maximum(m_old, jnp.max(s, axis=1, keepdims=True))
            alpha = jnp.exp2((m_old - m_new) * cexp)
            p = jnp.exp2((s - m_new) * cexp)
            m_ref[hd] = m_new
            l_ref[hd] = l_ref[hd] * alpha + jnp.sum(p, axis=1, keepdims=True)
            acc_ref[hd] = acc_ref[hd] * alpha + jnp.dot(p.astype(BF16), v, preferred_element_type=F32)
        return carry

    lax.fori_loop(0, (own + MOBA_GROUP - 1) // MOBA_GROUP, body, 0)
    for gp in range(MOBA_HEADS // 2):
        lo = acc_ref[2 * gp] / l_ref[2 * gp]
        hi = acc_ref[2 * gp + 1] / l_ref[2 * gp + 1]
        o_ref[:, gp * LANE:(gp + 1) * LANE] = jnp.where(lane < MOBA_DIM, lo, hi).astype(BF16)


def _moba_prompt(qaug, kaug, vmb, batch, seq):
    tq = MOBA_BLOCK
    nq = seq // tq
    assert seq % (MOBA_GROUP * MOBA_BLOCK) == 0
    return pl.pallas_call(
        _moba_prompt_kernel,
        out_shape=jax.ShapeDtypeStruct((batch * seq, MOBA_WIDTH), BF16),
        grid=(batch, nq),
        in_specs=[pl.BlockSpec((tq, MOBA_HEADS * LANE), lambda b, i: (b * nq + i, 0)),
                  pl.BlockSpec((seq, MOBA_HEADS * LANE), lambda b, i: (b, 0)),
                  pl.BlockSpec((seq, MOBA_WIDTH), lambda b, i: (b, 0))],
        out_specs=pl.BlockSpec((tq, MOBA_WIDTH), lambda b, i: (b * nq + i, 0)),
        scratch_shapes=[pltpu.VMEM((MOBA_HEADS, tq, 1), F32), pltpu.VMEM((MOBA_HEADS, tq, 1), F32),
                        pltpu.VMEM((MOBA_HEADS, tq, LANE), F32)],
        compiler_params=_cparams(("arbitrary", "arbitrary")),
        name="moba_prompt",
    )(qaug, kaug, vmb)


def _attn_out_kernel(x_ref, omla_ref, omoba_ref, wv_ref, wout_ref, g_ref, b_ref, y_ref):
    tm = x_ref.shape[0]
    x = x_ref[...]
    mix = _bdot(omoba_ref[...], wout_ref[MLA_HEADS * V_DIM:, :])
    for gp in range(MLA_HEADS // 2):
        v2 = jnp.zeros((tm, LANE), F32)
        for e in range(2):
            hd = 2 * gp + e
            o = omla_ref[:, hd].reshape(tm, KV_RANK)
            v2 = v2 + jnp.dot(o, wv_ref[hd], preferred_element_type=F32)
        mix = mix + _bdot(v2, wout_ref[gp * LANE:(gp + 1) * LANE, :])
    y_ref[...] = _layer_norm(ALPHA * x + mix, g_ref[...], b_ref[...])


def _attn_out(x, omla, omoba, w, g, b):
    n = x.shape[0]
    tm = ROW_TILE
    nsub = tm // LANE
    full = lambda a: pl.BlockSpec(a.shape, lambda i: (0,) * a.ndim)
    return pl.pallas_call(
        _attn_out_kernel,
        out_shape=jax.ShapeDtypeStruct((n, D_MODEL), F32),
        grid=(n // tm,),
        in_specs=[pl.BlockSpec((tm, D_MODEL), lambda i: (i, 0)),
                  pl.BlockSpec((nsub, MLA_HEADS, LANE, KV_RANK), lambda i: (i, 0, 0, 0)),
                  pl.BlockSpec((tm, MOBA_WIDTH), lambda i: (i, 0)),
                  full(w["wv"]), full(w["w_out"]), full(g), full(b)],
        out_specs=pl.BlockSpec((tm, D_MODEL), lambda i: (i, 0)),
        compiler_params=_cparams(("arbitrary",)),
        name="attn_out",
    )(x, omla, omoba, w["wv"], w["w_out"], g, b)


def _page_copies(pt_ref, pool_ref, buf_ref, sem_ref, layer, first_page, slot):
    return [pltpu.make_async_copy(pool_ref.at[layer, pt_ref[first_page + i]], buf_ref.at[slot, i], sem_ref.at[slot])
            for i in range(PAGES_PER_STEP)]


def _mla_sample_kernel(pt_ref, qlat_ref, qrope_ref, latn_ref, ropen_ref, lat_hbm, rope_hbm,
                       o_ref, latbuf, ropebuf, latsem, ropesem, m_ref, l_ref, acc_ref, *, layer, n_pages):
    pps = PAGES_PER_STEP
    j = pl.program_id(1)
    steps = n_pages // pps
    g = pl.program_id(0) * steps + j
    slot = g % 2
    cexp = MLA_SCALE * LOG2E
    ql = qlat_ref[0]
    qr = qrope_ref[0]
    rows = ql.shape[0]

    def copies(step, slot_):
        first = (step // steps) * n_pages + (step % steps) * pps
        return (_page_copies(pt_ref, lat_hbm, latbuf, latsem, layer, first, slot_)
                + _page_copies(pt_ref, rope_hbm, ropebuf, ropesem, layer, first, slot_))

    @pl.when(g == 0)
    def _():
        for c in copies(g, slot):
            c.start()

    @pl.when(g + 1 < pl.num_programs(0) * steps)
    def _():
        for c in copies(g + 1, 1 - slot):
            c.start()

    for c in copies(g, slot):
        c.wait()

    @pl.when(j == 0)
    def _():
        m_ref[...] = jnp.full((rows, 1), NEG, F32)
        l_ref[...] = jnp.zeros((rows, 1), F32)
        acc_ref[...] = jnp.zeros((rows, KV_RANK), F32)

    lat = jnp.concatenate([latbuf[slot, i] for i in range(pps)], axis=0).astype(BF16)
    rpt = jnp.concatenate([ropebuf[slot, i] for i in range(pps)], axis=1)
    s = _bdot_nt(ql, lat) + _bdot(qr, rpt)
    m_old = m_ref[...]
    m_new = jnp.maximum(m_old, jnp.max(s, axis=1, keepdims=True))
    alpha = jnp.exp2((m_old - m_new) * cexp)
    p = jnp.exp2((s - m_new) * cexp)
    m_ref[...] = m_new
    l_ref[...] = l_ref[...] * alpha + jnp.sum(p, axis=1, keepdims=True)
    acc_ref[...] = acc_ref[...] * alpha + jnp.dot(p.astype(BF16), lat, preferred_element_type=F32)

    @pl.when(j == pl.num_programs(1) - 1)
    def _():
        latn = latn_ref[0].astype(BF16)
        s = _bdot_nt(ql, latn) + _bdot_nt(qr, ropen_ref[0])
        tq = lax.broadcasted_iota(jnp.int32, s.shape, 0) // MLA_HEADS
        tk = lax.broadcasted_iota(jnp.int32, s.shape, 1)
        s = jnp.where(tk <= tq, s, NEG)
        m_old = m_ref[...]
        m_new = jnp.maximum(m_old, jnp.max(s, axis=1, keepdims=True))
        alpha = jnp.exp2((m_old - m_new) * cexp)
        p = jnp.exp2((s - m_new) * cexp)
        l = l_ref[...] * alpha + jnp.sum(p, axis=1, keepdims=True)
        acc = acc_ref[...] * alpha + jnp.dot(p.astype(BF16), latn, preferred_element_type=F32)
        o_ref[0] = (acc / l).astype(BF16)


def _mla_sample(page_table, qlat, qrope, lat_new, rope_new, pool_lat, pool_rope_t, layer):
    nb, rows, _ = qlat.shape
    n_new = lat_new.shape[1]
    n_pages = page_table.shape[1]
    pps = PAGES_PER_STEP
    assert n_pages % pps == 0
    pt = page_table.reshape(-1)

    per_seq = lambda a: pl.BlockSpec((1,) + a.shape[1:], lambda b, j, pt_ref: (b, 0, 0))
    hbm = pl.BlockSpec(memory_space=pl.ANY)
    grid_spec = pltpu.PrefetchScalarGridSpec(
        num_scalar_prefetch=1,
        grid=(nb, n_pages // pps),
        in_specs=[per_seq(qlat), per_seq(qrope), per_seq(lat_new), per_seq(rope_new), hbm, hbm],
        out_specs=pl.BlockSpec((1, rows, KV_RANK), lambda b, j, pt_ref: (b, 0, 0)),
        scratch_shapes=[pltpu.VMEM((2, pps, PAGE_SIZE, KV_RANK), F32), pltpu.VMEM((2, pps, ROPE_DIM, PAGE_SIZE), F32),
                        pltpu.SemaphoreType.DMA((2,)), pltpu.SemaphoreType.DMA((2,)),
                        pltpu.VMEM((rows, 1), F32), pltpu.VMEM((rows, 1), F32), pltpu.VMEM((rows, KV_RANK), F32)],
    )
    return pl.pallas_call(
        functools.partial(_mla_sample_kernel, layer=layer, n_pages=n_pages),
        out_shape=jax.ShapeDtypeStruct((nb, rows, KV_RANK), BF16),
        grid_spec=grid_spec,
        compiler_params=_cparams(("arbitrary", "arbitrary")),
        name="mla_sample",
    )(pt, qlat, qrope, lat_new, rope_new, pool_lat, pool_rope_t)


def _moba_sample_kernel(pt_ref, qbd_ref, qbdf_ref, kn_ref, vn_ref, expand_ref, kt_hbm, vt_hbm,
                        o_ref, buf, sem, s_ref, ksum_ref, p_ref, pn_ref, den_ref, acc_ref,
                        *, n_new, n_past_blocks, layer, n_pages):
    pps = PAGES_PER_STEP
    phase = pl.program_id(1)
    j = pl.program_id(2)
    steps = n_pages // pps
    g = (pl.program_id(0) * 2 + phase) * steps + j
    slot = g % 2
    cexp = MOBA_SCALE * LOG2E

    def copies(pool_ref, step, slot_):
        first = (step // (2 * steps)) * n_pages + (step % steps) * pps
        return _page_copies(pt_ref, pool_ref, buf, sem, layer, first, slot_)

    def start(step, slot_):
        is_v = (step // steps) % 2

        @pl.when(is_v == 0)
        def _():
            for c in copies(kt_hbm, step, slot_):
                c.start()

        @pl.when(is_v == 1)
        def _():
            for c in copies(vt_hbm, step, slot_):
                c.start()

    @pl.when(g == 0)
    def _():
        start(g, slot)

    @pl.when(g + 1 < pl.num_programs(0) * 2 * steps)
    def _():
        start(g + 1, 1 - slot)

    for c in copies(kt_hbm, g, slot):
        c.wait()
    qbd = qbd_ref[0]
    pages_per_block = MOBA_BLOCK // PAGE_SIZE

    @pl.when((phase == 0) & (j == 0))
    def _():
        ksum_ref[...] = jnp.zeros(ksum_ref.shape, F32)

    @pl.when(phase == 0)
    def _():
        lane = lax.broadcasted_iota(jnp.int32, ksum_ref.shape, 1)
        for i0 in range(0, pps, pages_per_block):
            ksum = None
            for i in range(i0, i0 + pages_per_block):
                kt = buf[slot, i]
                col = pl.multiple_of((j * pps + i) * PAGE_SIZE, PAGE_SIZE)
                s_ref[:, pl.ds(col, PAGE_SIZE)] = _bdot(qbd, kt)
                ksum = kt if ksum is None else ksum + kt
            blk = (j * pps + i0) // pages_per_block
            ksum_ref[...] = jnp.where(lane == blk, jnp.sum(ksum, axis=1, keepdims=True), ksum_ref[...])

    @pl.when((phase == 1) & (j == 0))
    def _():
        kmean = ksum_ref[...] * (1.0 / MOBA_BLOCK)
        gate = jnp.dot(qbdf_ref[0], kmean, precision=lax.Precision.HIGHEST, preferred_element_type=F32)
        nidx = lax.broadcasted_iota(jnp.int32, gate.shape, 1)
        g = jnp.where(nidx < n_past_blocks, gate, -jnp.inf)
        nidx = nidx.astype(F32)
        picked = jnp.zeros(gate.shape, F32)
        for _ in range(MOBA_TOPK):
            top = jnp.max(g, axis=1, keepdims=True)
            first = jnp.min(jnp.where(g == top, nidx, float(LANE)), axis=1, keepdims=True)
            hit = nidx == first
            picked = picked + jnp.where(hit & (top > -jnp.inf), 1.0, 0.0)
            g = jnp.where(hit, -jnp.inf, g)
        chosen = jnp.dot(picked.astype(BF16), expand_ref[...], preferred_element_type=F32)
        s = jnp.where(chosen > 0.5, s_ref[...], NEG)
        sn = _bdot_nt(qbd, kn_ref[0])
        tq = lax.broadcasted_iota(jnp.int32, sn.shape, 0) // MOBA_HEADS
        tk = lax.broadcasted_iota(jnp.int32, sn.shape, 1)
        sn = jnp.where(tk <= tq, sn, NEG)
        m = jnp.maximum(jnp.max(s, axis=1, keepdims=True), jnp.max(sn, axis=1, keepdims=True))
        p = jnp.exp2((s - m) * cexp)
        pn = jnp.exp2((sn - m) * cexp)
        den_ref[...] = jnp.sum(p, axis=1, keepdims=True) + jnp.sum(pn, axis=1, keepdims=True)
        p_ref[...] = p.astype(BF16)
        pn_ref[...] = pn
        acc_ref[...] = jnp.zeros(acc_ref.shape, F32)

    @pl.when(phase == 1)
    def _():
        acc = acc_ref[...]
        for i in range(pps):
            col = pl.multiple_of((j * pps + i) * PAGE_SIZE, PAGE_SIZE)
            acc = acc + _bdot_nt(p_ref[:, pl.ds(col, PAGE_SIZE)], buf[slot, i])
        acc_ref[...] = acc

    @pl.when((phase == 1) & (j == pl.num_programs(2) - 1))
    def _():
        o = (acc_ref[...] + _bdot(pn_ref[...], vn_ref[0])) / den_ref[...]
        hrow = lax.broadcasted_iota(jnp.int32, o.shape, 0) % MOBA_HEADS
        hcol = lax.broadcasted_iota(jnp.int32, o.shape, 1) // MOBA_DIM
        o = jnp.where(hrow == hcol, o, 0.0)
        for t in range(n_new):
            o_ref[0, t:t + 1, :] = jnp.sum(o[t * MOBA_HEADS:(t + 1) * MOBA_HEADS], axis=0,
                                           keepdims=True).astype(BF16)


def _moba_sample(page_table, qbd, qbdf, k_new, v_new, pool_kt, pool_vt, layer):
    nb, rows, _ = qbd.shape
    n_new = k_new.shape[1]
    n_pages = page_table.shape[1]
    pps = PAGES_PER_STEP
    past_len = n_pages * PAGE_SIZE
    pages_per_block = MOBA_BLOCK // PAGE_SIZE
    assert n_pages % pps == 0 and pps % pages_per_block == 0 and past_len % MOBA_BLOCK == 0 and n_new <= MOBA_BLOCK
    n_past_blocks = past_len // MOBA_BLOCK
    assert n_past_blocks <= LANE
    steps = n_pages // pps
    pt = page_table.reshape(-1)
    expand = (jnp.arange(past_len)[None, :] // MOBA_BLOCK == jnp.arange(LANE)[:, None]).astype(BF16)

    per_seq = lambda a: pl.BlockSpec((1,) + a.shape[1:], lambda b, ph, j, pt_ref: (b, 0, 0))
    hbm = pl.BlockSpec(memory_space=pl.ANY)
    grid_spec = pltpu.PrefetchScalarGridSpec(
        num_scalar_prefetch=1,
        grid=(nb, 2, steps),
        in_specs=[per_seq(qbd), per_seq(qbdf), per_seq(k_new), per_seq(v_new),
                  pl.BlockSpec(expand.shape, lambda b, ph, j, pt_ref: (0, 0)), hbm, hbm],
        out_specs=pl.BlockSpec((1, n_new, MOBA_WIDTH), lambda b, ph, j, pt_ref: (b, 0, 0)),
        scratch_shapes=[pltpu.VMEM((2, pps, MOBA_WIDTH, PAGE_SIZE), F32),
                        pltpu.SemaphoreType.DMA((2,)),
                        pltpu.VMEM((rows, past_len), F32),
                        pltpu.VMEM((MOBA_WIDTH, LANE), F32),
                        pltpu.VMEM((rows, past_len), BF16),
                        pltpu.VMEM((rows, n_new), F32),
                        pltpu.VMEM((rows, 1), F32),
                        pltpu.VMEM((rows, MOBA_WIDTH), F32)],
    )
    return pl.pallas_call(
        functools.partial(_moba_sample_kernel, n_new=n_new, n_past_blocks=n_past_blocks, layer=layer,
                          n_pages=n_pages),
        out_shape=jax.ShapeDtypeStruct((nb, n_new, MOBA_WIDTH), BF16),
        grid_spec=grid_spec,
        compiler_params=_cparams(("arbitrary", "arbitrary", "arbitrary")),
        name="moba_sample",
    )(pt, qbd, qbdf, k_new, v_new, expand, pool_kt, pool_vt)


def _s5_param_kernel(are_c_ref, aim_c_ref, dt_c_ref, bre_ref, bim_ref, are_r_ref, aim_r_ref, dt_r_ref,
                     bbre_ref, bbim_ref, pw_ref, step_ref):
    def abar(are, aim, dt):
        mag = jnp.exp(are * dt)
        return mag * jnp.cos(aim * dt), mag * jnp.sin(aim * dt)

    are, aim = are_c_ref[...], aim_c_ref[...]
    ar, ai = abar(are, aim, dt_c_ref[...])
    nr, ni = ar - 1.0, ai
    den = are * are + aim * aim
    cr = (nr * are + ni * aim) / den
    ci = (ni * are - nr * aim) / den
    bre, bim = bre_ref[...], bim_ref[...]
    bbre_ref[...] = cr * bre - ci * bim
    bbim_ref[...] = cr * bim + ci * bre
    ar, ai = abar(are_r_ref[...], aim_r_ref[...], dt_r_ref[...])
    pr, pi = ar, ai
    powers = []
    for i in range(SUBLANE):
        pw_ref[0, i:i + 1, :] = pr
        pw_ref[1, i:i + 1, :] = pi
        powers.append((pr, pi))
        pr, pi = pr * ar - pi * ai, pr * ai + pi * ar
    zero = jnp.zeros_like(ar)
    for k in range(3):
        d = 1 << k
        for i in range(SUBLANE):
            step_ref[2 * k, i:i + 1, :] = powers[d - 1][0] if i >= d else zero
            step_ref[2 * k + 1, i:i + 1, :] = powers[d - 1][1] if i >= d else zero


def _s5_params(a_re, a_im, log_dt, b_re, b_im, c_re, c_im):
    col = lambda a: a.reshape(SSM_CH, 1)
    rowv = lambda a: a.reshape(1, SSM_CH)
    dt = jnp.exp(log_dt.astype(F32))
    dt_full = jnp.broadcast_to(dt[:, None], (SSM_GROUPS, SSM_STATE))
    vm = lambda shape: pl.BlockSpec(shape, lambda: (0,) * len(shape))
    ins = [col(a_re), col(a_im), col(dt_full), b_re.reshape(SSM_CH, SSM_GROUP), b_im.reshape(SSM_CH, SSM_GROUP),
           rowv(a_re), rowv(a_im), rowv(dt_full)]
    bbre, bbim, pw, step = pl.pallas_call(
        _s5_param_kernel,
        out_shape=[jax.ShapeDtypeStruct((SSM_CH, SSM_GROUP), F32), jax.ShapeDtypeStruct((SSM_CH, SSM_GROUP), F32),
                   jax.ShapeDtypeStruct((2, SUBLANE, SSM_CH), F32), jax.ShapeDtypeStruct((6, SUBLANE, SSM_CH), F32)],
        in_specs=[vm(a.shape) for a in ins],
        out_specs=[vm((SSM_CH, SSM_GROUP)), vm((SSM_CH, SSM_GROUP)), vm((2, SUBLANE, SSM_CH)), vm((6, SUBLANE, SSM_CH))],
        compiler_params=pltpu.CompilerParams(vmem_limit_bytes=VMEM_LIMIT),
        name="s5_params",
    )(*ins)
    gpc = SSM_GROUPS // S5_CHUNKS
    eye = jnp.eye(gpc, dtype=F32)

    def in_mat(bb):
        b4 = bb.reshape(S5_CHUNKS, gpc, SSM_STATE, SSM_GROUP)
        return jnp.einsum("kgpc,gh->kgchp", b4, eye).reshape(S5_CHUNKS, S5_CIN, S5_CST)

    def out_mat(cc):
        c4 = cc.reshape(S5_CHUNKS, gpc, SSM_GROUP, SSM_STATE)
        return jnp.einsum("kgcp,gh->kgphc", c4, eye).reshape(S5_CHUNKS, S5_CST, S5_CIN)

    wb = jnp.concatenate([in_mat(bbre), in_mat(bbim)], axis=-1).astype(BF16)
    return dict(wb=wb, wc_re=out_mat(c_re.astype(F32)).astype(BF16), wc_im=out_mat(c_im.astype(F32)).astype(BF16),
                pw=pw, step=step)


def _s5_tail(x, y, d_ref, wglu_ref, g_ref, b_ref):
    z = _gelu(y + d_ref[...] * x)
    zz = _bdot(z, wglu_ref[...])
    out = zz[:, :D_MODEL] * _sigmoid(zz[:, D_MODEL:])
    return _layer_norm(ALPHA * x + out, g_ref[...], b_ref[...])


def _s5_prompt_kernel(x_ref, wb_ref, wcre_ref, wcim_ref, pw_ref, step_ref, d_ref, wglu_ref, g_ref, b_ref,
                      y_ref, hl_ref, carry_ref, h_scr):
    tt = x_ref.shape[0]

    @pl.when(pl.program_id(1) == 0)
    def _():
        carry_ref[...] = jnp.zeros(carry_ref.shape, F32)

    x = x_ref[...]
    ys = []
    for k in range(S5_CHUNKS):
        lo = k * S5_CST
        h_scr[...] = _bdot(x[:, k * S5_CIN:(k + 1) * S5_CIN], wb_ref[k])
        pr, pi = pw_ref[0, :, lo:lo + S5_CST], pw_ref[1, :, lo:lo + S5_CST]
        steps = [(step_ref[2 * s, :, lo:lo + S5_CST], step_ref[2 * s + 1, :, lo:lo + S5_CST], 1 << s)
                 for s in range(3)]

        def block(i, carry, pr=pr, pi=pi, steps=steps):
            cr, ci = carry
            r0 = pl.multiple_of(i * SUBLANE, SUBLANE)
            hr = h_scr[pl.ds(r0, SUBLANE), 0:S5_CST]
            hi = h_scr[pl.ds(r0, SUBLANE), S5_CST:2 * S5_CST]
            for sr, si, sh in steps:
                rr, ri = pltpu.roll(hr, sh, axis=0), pltpu.roll(hi, sh, axis=0)
                hr, hi = hr + sr * rr - si * ri, hi + sr * ri + si * rr
            hr, hi = hr + pr * cr - pi * ci, hi + pr * ci + pi * cr
            h_scr[pl.ds(r0, SUBLANE), 0:S5_CST] = hr
            h_scr[pl.ds(r0, SUBLANE), S5_CST:2 * S5_CST] = hi
            return hr[SUBLANE - 1:SUBLANE], hi[SUBLANE - 1:SUBLANE]

        cr, ci = lax.fori_loop(0, tt // SUBLANE, block,
                               (carry_ref[k, 0:1, 0:S5_CST], carry_ref[k, 0:1, S5_CST:2 * S5_CST]))
        carry_ref[k, 0:1, 0:S5_CST] = cr
        carry_ref[k, 0:1, S5_CST:2 * S5_CST] = ci
        ys.append(_bdot(h_scr[:, 0:S5_CST], wcre_ref[k]) - _bdot(h_scr[:, S5_CST:2 * S5_CST], wcim_ref[k]))
    y_ref[...] = _s5_tail(x, jnp.concatenate(ys, axis=-1), d_ref, wglu_ref, g_ref, b_ref)
    hl_ref[0] = carry_ref[...]


def _s5_prompt(x, sp, d, wglu, g, b, batch, seq):
    tt = ROW_TILE
    nt = seq // tt
    full = lambda a: pl.BlockSpec(a.shape, lambda bb, i: (0,) * a.ndim)
    y, hl = pl.pallas_call(
        _s5_prompt_kernel,
        out_shape=[jax.ShapeDtypeStruct((batch * seq, D_MODEL), F32),
                   jax.ShapeDtypeStruct((batch, S5_CHUNKS, SUBLANE, 2 * S5_CST), F32)],
        grid=(batch, nt),
        in_specs=[pl.BlockSpec((tt, D_MODEL), lambda bb, i: (bb * nt + i, 0)),
                  full(sp["wb"]), full(sp["wc_re"]), full(sp["wc_im"]), full(sp["pw"]), full(sp["step"]),
                  full(d), full(wglu), full(g), full(b)],
        out_specs=[pl.BlockSpec((tt, D_MODEL), lambda bb, i: (bb * nt + i, 0)),
                   pl.BlockSpec((1, S5_CHUNKS, SUBLANE, 2 * S5_CST), lambda bb, i: (bb, 0, 0, 0))],
        scratch_shapes=[pltpu.VMEM((S5_CHUNKS, SUBLANE, 2 * S5_CST), F32), pltpu.VMEM((tt, 2 * S5_CST), F32)],
        compiler_params=_cparams(("arbitrary", "arbitrary")),
        name="s5_prompt",
    )(x, sp["wb"], sp["wc_re"], sp["wc_im"], sp["pw"], sp["step"], d, wglu, g, b)
    h_re = hl[:, :, 0, :S5_CST].reshape(batch, SSM_GROUPS, SSM_STATE)
    h_im = hl[:, :, 0, S5_CST:].reshape(batch, SSM_GROUPS, SSM_STATE)
    return y, h_re, h_im


def _s5_sample_kernel(x_ref, h0re_ref, h0im_ref, wb_ref, wcre_ref, wcim_ref, pw_ref, d_ref, wglu_ref, g_ref, b_ref,
                      y_ref, hre_ref, him_ref):
    @pl.when(pl.program_id(0) == 0)
    def _():
        hre_ref[...] = h0re_ref[...]
        him_ref[...] = h0im_ref[...]

    x = x_ref[...]
    ys = []
    for k in range(S5_CHUNKS):
        lo = k * S5_CST
        bu = _bdot(x[:, k * S5_CIN:(k + 1) * S5_CIN], wb_ref[k])
        ar, ai = pw_ref[0, 0:1, lo:lo + S5_CST], pw_ref[1, 0:1, lo:lo + S5_CST]
        hr, hi = hre_ref[:, lo:lo + S5_CST], him_ref[:, lo:lo + S5_CST]
        nr = ar * hr - ai * hi + bu[:, :S5_CST]
        ni = ar * hi + ai * hr + bu[:, S5_CST:]
        hre_ref[:, lo:lo + S5_CST] = nr
        him_ref[:, lo:lo + S5_CST] = ni
        ys.append(_bdot(nr, wcre_ref[k]) - _bdot(ni, wcim_ref[k]))
    y_ref[...] = _s5_tail(x, jnp.concatenate(ys, axis=-1), d_ref, wglu_ref, g_ref, b_ref)


def _s5_sample(x, h0_re, h0_im, sp, d, wglu, g, b, nb, n_new):
    full = lambda a: pl.BlockSpec(a.shape, lambda t: (0,) * a.ndim)
    state = pl.BlockSpec((nb, SSM_CH), lambda t: (0, 0))
    y, h_re, h_im = pl.pallas_call(
        _s5_sample_kernel,
        out_shape=[jax.ShapeDtypeStruct((n_new * nb, D_MODEL), F32),
                   jax.ShapeDtypeStruct((nb, SSM_CH), F32), jax.ShapeDtypeStruct((nb, SSM_CH), F32)],
        grid=(n_new,),
        in_specs=[pl.BlockSpec((nb, D_MODEL), lambda t: (t, 0)), state, state,
                  full(sp["wb"]), full(sp["wc_re"]), full(sp["wc_im"]), full(sp["pw"]),
                  full(d), full(wglu), full(g), full(b)],
        out_specs=[pl.BlockSpec((nb, D_MODEL), lambda t: (t, 0)), state, state],
        compiler_params=_cparams(("arbitrary",)),
        name="s5_sample",
    )(x, h0_re.reshape(nb, SSM_CH), h0_im.reshape(nb, SSM_CH), sp["wb"], sp["wc_re"], sp["wc_im"], sp["pw"],
      d, wglu, g, b)
    return y, h_re.reshape(nb, SSM_GROUPS, SSM_STATE), h_im.reshape(nb, SSM_GROUPS, SSM_STATE)


def _ffn_prompt_kernel(x_ref, wup_ref, cw_ref, cb_ref, wdn_ref, g_ref, b_ref, y_ref, st_ref,
                       carry_ref, ua_ref, ub_ref, act_ref):
    tm = x_ref.shape[0]

    @pl.when(pl.program_id(1) == 0)
    def _():
        carry_ref[...] = jnp.zeros(carry_ref.shape, F32)

    x = x_ref[...]
    xb = x.astype(BF16)

    def conv(col0, scr):
        cols = slice(col0, col0 + FFN_CW)
        u = jnp.dot(xb, wup_ref[:, cols], preferred_element_type=F32)
        scr[0:SUBLANE, :] = carry_ref[:, cols]
        scr[SUBLANE:SUBLANE + tm, :] = u
        carry_ref[:, cols] = u[tm - SUBLANE:tm]
        p1 = scr[SUBLANE - 1:SUBLANE - 1 + tm, :]
        p2 = scr[SUBLANE - 2:SUBLANE - 2 + tm, :]
        return cb_ref[:, cols] + cw_ref[0:1, cols] * p2 + cw_ref[1:2, cols] * p1 + cw_ref[2:3, cols] * u

    for c in range(D_FF // FFN_CW):
        act = _gelu(conv(c * FFN_CW, ua_ref)) * conv(D_FF + c * FFN_CW, ub_ref)
        act_ref[:, c * FFN_CW:(c + 1) * FFN_CW] = act.astype(BF16)
    down = jnp.dot(act_ref[...], wdn_ref[...], preferred_element_type=F32)
    y_ref[...] = _layer_norm(ALPHA * x + down, g_ref[...], b_ref[...])
    st_ref[0] = carry_ref[...]


def _ffn_prompt(x, w, g, b, batch, seq):
    tm = ROW_TILE
    nt = seq // tm
    assert CONV_W - 1 <= SUBLANE and D_FF % FFN_CW == 0
    full = lambda a: pl.BlockSpec(a.shape, lambda bb, i: (0,) * a.ndim)
    y, st = pl.pallas_call(
        _ffn_prompt_kernel,
        out_shape=[jax.ShapeDtypeStruct((batch * seq, D_MODEL), F32),
                   jax.ShapeDtypeStruct((batch, SUBLANE, 2 * D_FF), F32)],
        grid=(batch, nt),
        in_specs=[pl.BlockSpec((tm, D_MODEL), lambda bb, i: (bb * nt + i, 0)),
                  full(w["w_up"]), full(w["conv_w"]), full(w["conv_b"]), full(w["w_down"]), full(g), full(b)],
        out_specs=[pl.BlockSpec((tm, D_MODEL), lambda bb, i: (bb * nt + i, 0)),
                   pl.BlockSpec((1, SUBLANE, 2 * D_FF), lambda bb, i: (bb, 0, 0))],
        scratch_shapes=[pltpu.VMEM((SUBLANE, 2 * D_FF), F32),
                        pltpu.VMEM((SUBLANE + tm, FFN_CW), F32), pltpu.VMEM((SUBLANE + tm, FFN_CW), F32),
                        pltpu.VMEM((tm, D_FF), BF16)],
        compiler_params=_cparams(("arbitrary", "arbitrary")),
        name="ffn_prompt",
    )(x, w["w_up"], w["conv_w"], w["conv_b"], w["w_down"], g, b)
    return y, st[:, SUBLANE - (CONV_W - 1):, :]


def _ffn_sample_kernel(x_ref, st_ref, wup_ref, cw_ref, cb_ref, wdn_ref, g_ref, b_ref, y_ref, ns_ref, act_ref):
    tm = x_ref.shape[0]
    w2 = 2 * D_FF

    @pl.when(pl.program_id(0) == 0)
    def _():
        ns_ref[...] = st_ref[...]

    x = x_ref[...]
    xb = x.astype(BF16)

    def conv(col0):
        cols = slice(col0, col0 + FFN_CW)
        cols1 = slice(w2 + col0, w2 + col0 + FFN_CW)
        u = jnp.dot(xb, wup_ref[:, cols], preferred_element_type=F32)
        p2 = ns_ref[:, cols]
        p1 = ns_ref[:, cols1]
        ns_ref[:, cols] = p1
        ns_ref[:, cols1] = u
        return cb_ref[:, cols] + cw_ref[0:1, cols] * p2 + cw_ref[1:2, cols] * p1 + cw_ref[2:3, cols] * u

    for c in range(D_FF // FFN_CW):
        act = _gelu(conv(c * FFN_CW)) * conv(D_FF + c * FFN_CW)
        act_ref[:, c * FFN_CW:(c + 1) * FFN_CW] = act.astype(BF16)
    down = jnp.dot(act_ref[...], wdn_ref[...], preferred_element_type=F32)
    y_ref[...] = _layer_norm(ALPHA * x + down, g_ref[...], b_ref[...])


def _ffn_sample(x, state, w, g, b, nb, n_new):
    assert CONV_W == 3
    full = lambda a: pl.BlockSpec(a.shape, lambda t: (0,) * a.ndim)
    st = state.reshape(nb, (CONV_W - 1) * 2 * D_FF)
    y, ns = pl.pallas_call(
        _ffn_sample_kernel,
        out_shape=[jax.ShapeDtypeStruct((n_new * nb, D_MODEL), F32), jax.ShapeDtypeStruct(st.shape, F32)],
        grid=(n_new,),
        in_specs=[pl.BlockSpec((nb, D_MODEL), lambda t: (t, 0)), full(st),
                  full(w["w_up"]), full(w["conv_w"]), full(w["conv_b"]), full(w["w_down"]), full(g), full(b)],
        out_specs=[pl.BlockSpec((nb, D_MODEL), lambda t: (t, 0)), full(st)],
        scratch_shapes=[pltpu.VMEM((nb, D_FF), BF16)],
        compiler_params=_cparams(("arbitrary",)),
        name="ffn_sample",
    )(x, st, w["w_up"], w["conv_w"], w["conv_b"], w["w_down"], g, b)
    return y, ns.reshape(nb, CONV_W - 1, 2 * D_FF)


def _attn_weights(w_in, q_norm, w_q_b, kv_norm, w_kv_b, w_out):
    o1 = Q_RANK
    o2 = o1 + KV_RANK
    o3 = o2 + ROPE_DIM
    kpe = w_in[:, o2:o3]
    w_in_r = jnp.concatenate([w_in[:, :o2], w_in[:, o3:]] + [kpe] * (LANE // ROPE_DIM), axis=1).astype(BF16)
    assert w_in_r.shape[1] == W_IN_COLS
    qb = w_q_b.reshape(Q_RANK, MLA_HEADS, NOPE_DIM + ROPE_DIM)
    nope = jnp.pad(qb[:, :, :NOPE_DIM], ((0, 0), (0, 0), (0, LANE - NOPE_DIM))).reshape(Q_RANK, MLA_HEADS * LANE)
    ropew = qb[:, :, NOPE_DIM:].reshape(Q_RANK, MLA_HEADS * ROPE_DIM)
    w_q_b_r = jnp.concatenate([nope, ropew], axis=1).astype(BF16)
    wk = jnp.transpose(w_kv_b[:, :, :NOPE_DIM], (1, 2, 0))
    wk = jnp.pad(wk, ((0, 0), (0, LANE - NOPE_DIM), (0, 0))).astype(BF16)
    wv = jnp.transpose(w_kv_b[:, :, NOPE_DIM:], (1, 0, 2))
    wv_lo = jnp.pad(wv, ((0, 0), (0, 0), (0, LANE - V_DIM)))
    wv_hi = jnp.pad(wv, ((0, 0), (0, 0), (LANE - V_DIM, 0)))
    odd = (jnp.arange(MLA_HEADS) % 2 == 1)[:, None, None]
    wvp = jnp.where(odd, wv_hi, wv_lo).astype(BF16)
    return dict(w_in=w_in_r, q_norm=q_norm.reshape(1, Q_RANK), w_q_b=w_q_b_r, wk=wk,
                kv_norm=kv_norm.reshape(1, KV_RANK), wv=wvp, w_out=w_out.astype(BF16))


def _ffn_weights(w_up, conv_w, conv_b, w_down):
    return dict(w_up=w_up.astype(BF16), conv_w=conv_w, conv_b=conv_b.reshape(1, 2 * D_FF), w_down=w_down.astype(BF16))


def _to_time_major(a):
    return jnp.swapaxes(a, 0, 1).reshape((a.shape[0] * a.shape[1],) + a.shape[2:])


def _from_time_major(a, nb):
    return jnp.swapaxes(a.reshape((a.shape[0] // nb, nb) + a.shape[1:]), 0, 1)


def kernel(x_prompt, x_sample, cache_mla_latent, cache_mla_rope, cache_moba_k, cache_moba_v, state_s5_re, state_s5_im, state_ffn_conv, page_table, ln_mix_g, ln_mix_b, ln_ffn_g, ln_ffn_b, att_w_in, mla_q_norm, mla_w_q_b, mla_kv_norm, mla_w_kv_b, att_w_out, s5_a_re, s5_a_im, s5_log_dt, s5_b_re, s5_b_im, s5_c_re, s5_c_im, s5_d, s5_w_glu, ffn_w_up, ffn_conv_w, ffn_conv_b, ffn_w_down):
    batch, seq, _ = x_prompt.shape
    nb, n_new, _ = x_sample.shape
    past_len = page_table.shape[1] * PAGE_SIZE
    depth = ln_mix_g.shape[0]
    assert seq % MLA_TK == 0 and nb % SUBLANE == 0 and (nb * n_new) % ROW_TILE == 0

    xp = x_prompt.reshape(batch * seq, D_MODEL)
    xs = _to_time_major(x_sample)
    tabs_p = _rope_tables(seq, 0)
    tabs_s = [jnp.repeat(t[:n_new], nb, axis=0) for t in _rope_tables(SUBLANE * pl.cdiv(n_new, SUBLANE), past_len)]
    row = lambda v: v.reshape(1, -1)
    pool_rope_t = jnp.swapaxes(cache_mla_rope, 2, 3)

    def tokens_minor(pool):
        return jnp.transpose(pool, (0, 1, 3, 4, 2)).reshape(pool.shape[0], pool.shape[1], MOBA_WIDTH, PAGE_SIZE)

    pool_kt, pool_vt = tokens_minor(cache_moba_k), tokens_minor(cache_moba_v)

    outs_p ={k: [] for k in ("lat", "rope", "mk", "mv", "sre", "sim", "conv")}
    outs_s = {k: [] for k in ("lat", "rope", "mk", "mv", "sre", "sim", "conv")}
    for l in range(depth):
        g_mix, b_mix = row(ln_mix_g[l]), row(ln_mix_b[l])
        if l % 2 == 0:
            a = l // 2
            w = _attn_weights(att_w_in[a], mla_q_norm[a], mla_w_q_b[a], mla_kv_norm[a], mla_w_kv_b[a], att_w_out[a])
            qcat, kcat, lat, rp, qm, km, kaug, vm, vmb, kmean = _attn_proj(xp, w, tabs_p, seq)
            o_mla = _mla_prompt(qcat, kcat, batch, seq)
            qaug = _moba_select(qm, kmean, batch, seq)
            o_moba = _moba_prompt(qaug, kaug, vmb, batch, seq)
            xp = _attn_out(xp, o_mla, o_moba, w, g_mix, b_mix)
            outs_p["lat"].append(lat.reshape(batch, seq, KV_RANK))
            outs_p["rope"].append(rp.reshape(batch, seq, ROPE_DIM))
            outs_p["mk"].append(km.reshape(batch, seq, MOBA_HEADS, MOBA_DIM))
            outs_p["mv"].append(vm.reshape(batch, seq, MOBA_HEADS, MOBA_DIM))
            qcat, _, lat, rp, qm, km, _, vm, _, _ = _attn_proj(xs, w, tabs_s, nb * n_new)
            lat_b, rp_b = _from_time_major(lat, nb), _from_time_major(rp, nb)
            km_b, vm_b = _from_time_major(km, nb), _from_time_major(vm, nb)
            qm_b = _from_time_major(qm, nb)
            qc = qcat.reshape(n_new, nb // LANE, MLA_HEADS, LANE, 2 * LANE)
            qc = jnp.transpose(qc, (1, 3, 0, 2, 4)).reshape(nb, n_new * MLA_HEADS, 2 * LANE)
            qlat = qc[:, :, :KV_RANK]
            qr4 = qc[:, :, KV_RANK:].reshape(nb, n_new, MLA_HEADS, LANE // ROPE_DIM, ROPE_DIM)
            grp = (jnp.arange(MLA_HEADS) % (LANE // ROPE_DIM))
            qrope = jnp.take_along_axis(qr4, grp[None, None, :, None, None], axis=3)[:, :, :, 0, :]
            qrope = qrope.reshape(nb, n_new * MLA_HEADS, ROPE_DIM)
            o_mla = _mla_sample(page_table, qlat, qrope, lat_b, rp_b.astype(BF16), cache_mla_latent, pool_rope_t, a)
            q4 = qm_b.reshape(nb, n_new, MOBA_HEADS, MOBA_DIM)
            eye = jnp.eye(MOBA_HEADS, dtype=F32)
            qbdf = jnp.einsum("bthd,hk->bthkd", q4, eye).reshape(nb, n_new * MOBA_HEADS, MOBA_WIDTH)
            o_moba = _moba_sample(page_table, qbdf.astype(BF16), qbdf, km_b.astype(BF16), vm_b.astype(BF16),
                                  pool_kt, pool_vt, a)
            om = o_mla.reshape(nb // LANE, LANE, n_new, MLA_HEADS, KV_RANK)
            om = jnp.transpose(om, (2, 0, 3, 1, 4)).reshape(n_new * nb // LANE, MLA_HEADS, LANE, KV_RANK)
            xs = _attn_out(xs, om, _to_time_major(o_moba), w, g_mix, b_mix)
            outs_s["lat"].append(lat_b)
            outs_s["rope"].append(rp_b)
            outs_s["mk"].append(km_b.reshape(nb, n_new, MOBA_HEADS, MOBA_DIM))
            outs_s["mv"].append(vm_b.reshape(nb, n_new, MOBA_HEADS, MOBA_DIM))
        else:
            s = l // 2
            sp = _s5_params(s5_a_re[s], s5_a_im[s], s5_log_dt[s], s5_b_re[s], s5_b_im[s], s5_c_re[s], s5_c_im[s])
            d, wglu = row(s5_d[s]), s5_w_glu[s].astype(BF16)
            xp, hre, him = _s5_prompt(xp, sp, d, wglu, g_mix, b_mix, batch, seq)
            outs_p["sre"].append(hre)
            outs_p["sim"].append(him)
            xs, hre, him = _s5_sample(xs, state_s5_re[s], state_s5_im[s], sp, d, wglu, g_mix, b_mix, nb, n_new)
            outs_s["sre"].append(hre)
            outs_s["sim"].append(him)
        fw = _ffn_weights(ffn_w_up[l], ffn_conv_w[l], ffn_conv_b[l], ffn_w_down[l])
        g_ffn, b_ffn = row(ln_ffn_g[l]), row(ln_ffn_b[l])
        xp, cp = _ffn_prompt(xp, fw, g_ffn, b_ffn, batch, seq)
        xs, cs = _ffn_sample(xs, state_ffn_conv[l], fw, g_ffn, b_ffn, nb, n_new)
        outs_p["conv"].append(cp)
        outs_s["conv"].append(cs)

    keys = ("lat", "rope", "mk", "mv", "sre", "sim", "conv")
    return ((xp.reshape(batch, seq, D_MODEL), _from_time_major(xs, nb))
            + tuple(jnp.stack(outs_p[k]) for k in keys) + tuple(jnp.stack(outs_s[k]) for k in keys))
```

```python
import functools
import math

import jax
import jax.numpy as jnp
from jax import lax
from jax.experimental import pallas as pl
from jax.experimental.pallas import tpu as pltpu

F32 = jnp.float32
BF16 = jnp.bfloat16

D_MODEL = 1024
PAGE_SIZE = 128
MLA_HEADS = 8
Q_RANK = 256
KV_RANK = 128
NOPE_DIM = 64
ROPE_DIM = 32
V_DIM = 64
MLA_SCALE = (NOPE_DIM + ROPE_DIM) ** -0.5
MOBA_HEADS = 8
MOBA_DIM = 64
MOBA_WIDTH = MOBA_HEADS * MOBA_DIM
MOBA_BLOCK = 256
MOBA_TOPK = 3
MOBA_SCALE = MOBA_DIM ** -0.5
ROPE_THETA = 10000.0
SSM_GROUP = 16
SSM_GROUPS = D_MODEL // SSM_GROUP
SSM_STATE = 64
SSM_CH = SSM_GROUPS * SSM_STATE
D_FF = 2816
CONV_W = 3
DEPTH = 4
ALPHA = (2 * DEPTH) ** 0.25
LN_EPS = 1e-5
RMS_EPS = 1e-6

LANE = 128
SUBLANE = 8
NEG = -1e30
LOG2E = math.log2(math.e)
ROW_TILE = 256
MLA_TQ = 128
MLA_TK = 1024
MOBA_GROUP = 4
S5_CHUNKS = 4
S5_CIN = D_MODEL // S5_CHUNKS
S5_CST = SSM_CH // S5_CHUNKS
FFN_CW = 256
PAGES_PER_STEP = 16
PAGE_SLOTS = 3
W_IN_COLS = Q_RANK + KV_RANK + 3 * MOBA_WIDTH + LANE
VMEM_LIMIT = 56 * 1024 * 1024


def _cparams(sem):
    return pltpu.CompilerParams(dimension_semantics=sem, vmem_limit_bytes=VMEM_LIMIT)


def _bdot(a, b):
    return jnp.dot(a.astype(BF16), b.astype(BF16), preferred_element_type=F32)


def _bdot_nt(a, b):
    return lax.dot_general(a.astype(BF16), b.astype(BF16), (((1,), (1,)), ((), ())),
                           preferred_element_type=F32)


def _layer_norm(v, g, b):
    mu = jnp.mean(v, axis=-1, keepdims=True)
    d = v - mu
    var = jnp.mean(d * d, axis=-1, keepdims=True)
    return d * lax.rsqrt(var + LN_EPS) * g + b


def _rms_norm(v, g):
    return v * lax.rsqrt(jnp.mean(v * v, axis=-1, keepdims=True) + RMS_EPS) * g


def _gelu(v):
    return 0.5 * v * (1.0 + jnp.tanh(math.sqrt(2.0 / math.pi) * (v + 0.044715 * (v * v * v))))


def _sigmoid(v):
    return 1.0 / (1.0 + jnp.exp(-v))


def _swap_halves(v, half):
    w = v.shape[-1]
    lane = lax.broadcasted_iota(jnp.int32, v.shape, 1)
    fwd = pltpu.roll(v, w - half, axis=1)
    bwd = pltpu.roll(v, half, axis=1)
    return jnp.where((lane % (2 * half)) < half, fwd, bwd)


def _rope(v, cos, sin_signed, half):
    outs = []
    for c in range(v.shape[-1] // LANE):
        vc = v[:, c * LANE:(c + 1) * LANE]
        outs.append(vc * cos + _swap_halves(vc, half) * sin_signed)
    return outs[0] if len(outs) == 1 else jnp.concatenate(outs, axis=-1)


def _rope_table_kernel(inv64_ref, sg64_ref, inv32_ref, sg32_ref, c64_ref, s64_ref, c32_ref, s32_ref, *, offset):
    rows = c64_ref.shape[0]
    base = pl.program_id(0) * rows + offset
    pos = (base + lax.broadcasted_iota(jnp.int32, (rows, LANE), 0)).astype(F32)
    a64 = pos * inv64_ref[...]
    c64_ref[...] = jnp.cos(a64)
    s64_ref[...] = jnp.sin(a64) * sg64_ref[...]
    a32 = pos * inv32_ref[...]
    c32_ref[...] = jnp.cos(a32)
    s32_ref[...] = jnp.sin(a32) * sg32_ref[...]


def _rope_tables(n_rows, offset):
    def lane_consts(d):
        half = d // 2
        inv = ROPE_THETA ** (-jnp.arange(half, dtype=F32) * 2.0 / d)
        l = jnp.arange(LANE)
        return (inv[(l % d) % half].reshape(1, LANE),
                jnp.where((l % d) < half, -1.0, 1.0).astype(F32).reshape(1, LANE))
    inv64, sg64 = lane_consts(MOBA_DIM)
    inv32, sg32 = lane_consts(ROPE_DIM)
    rows = min(n_rows, 512)
    assert n_rows % rows == 0 and rows % SUBLANE == 0
    const = pl.BlockSpec((1, LANE), lambda i: (0, 0))
    out = pl.BlockSpec((rows, LANE), lambda i: (i, 0))
    return pl.pallas_call(
        functools.partial(_rope_table_kernel, offset=offset),
        out_shape=[jax.ShapeDtypeStruct((n_rows, LANE), F32)] * 4,
        grid=(n_rows // rows,),
        in_specs=[const] * 4,
        out_specs=[out] * 4,
        compiler_params=_cparams(("arbitrary",)),
        name="rope_tables",
    )(inv64, sg64, inv32, sg32)


def _proj_kernel(x_ref, win_ref, qn_ref, wqb_ref, wk_ref, kvn_ref, c64_ref, s64_ref, c32_ref, s32_ref,
                 qcat_ref, kcat_ref, lat_ref, rope_ref, qm_ref, km_ref, kaug_ref, vm_ref, vmb_ref, kmean_ref,
                 *, blocks_per_seq):
    tm = x_ref.shape[0]
    nsub = tm // LANE
    c64, s64, c32, s32 = c64_ref[...], s64_ref[...], c32_ref[...], s32_ref[...]
    h = _bdot(x_ref[...], win_ref[...])
    o_kv = Q_RANK
    o_qm = o_kv + KV_RANK
    o_km = o_qm + MOBA_WIDTH
    o_vm = o_km + MOBA_WIDTH
    o_pe = o_vm + MOBA_WIDTH

    qn = _rms_norm(h[:, :Q_RANK], qn_ref[...])
    q = _bdot(qn, wqb_ref[...])
    o_qr = MLA_HEADS * LANE
    qr = _rope(q[:, o_qr:o_qr + MLA_HEADS * ROPE_DIM], c32, s32, ROPE_DIM // 2)
    lane = lax.broadcasted_iota(jnp.int32, (tm, LANE), 1)
    heads_per_chunk = LANE // ROPE_DIM
    for hd in range(MLA_HEADS):
        ql = _bdot(q[:, hd * LANE:(hd + 1) * LANE], wk_ref[hd]).astype(BF16)
        grp = hd % heads_per_chunk
        chunk = qr[:, (hd // heads_per_chunk) * LANE:(hd // heads_per_chunk + 1) * LANE]
        qrh = jnp.where((lane >= grp * ROPE_DIM) & (lane < (grp + 1) * ROPE_DIM), chunk, 0.0).astype(BF16)
        for sb in range(nsub):
            qcat_ref[sb, hd, :, 0:LANE] = ql[sb * LANE:(sb + 1) * LANE]
            qcat_ref[sb, hd, :, LANE:2 * LANE] = qrh[sb * LANE:(sb + 1) * LANE]

    lat = _rms_norm(h[:, o_kv:o_kv + KV_RANK], kvn_ref[...])
    lat_ref[...] = lat
    kpe = _rope(h[:, o_pe:o_pe + LANE], c32, s32, ROPE_DIM // 2)
    rope_ref[...] = kpe[:, :ROPE_DIM]
    kcat_ref[:, 0:KV_RANK] = lat.astype(BF16)
    kcat_ref[:, KV_RANK:KV_RANK + LANE] = kpe.astype(BF16)

    qm_ref[...] = _rope(h[:, o_qm:o_qm + MOBA_WIDTH], c64, s64, MOBA_DIM // 2)
    km = _rope(h[:, o_km:o_km + MOBA_WIDTH], c64, s64, MOBA_DIM // 2)
    km_ref[...] = km
    kmean_ref[0] = jnp.mean(km, axis=0, keepdims=True)
    vm = h[:, o_vm:o_vm + MOBA_WIDTH]
    vm_ref[...] = vm
    vmb_ref[...] = vm.astype(BF16)
    blk = pl.program_id(0) % blocks_per_seq
    onehot = jnp.where(lane == MOBA_DIM + blk, 1.0, 0.0)
    for g in range(MOBA_HEADS // 2):
        ch = km[:, g * LANE:(g + 1) * LANE]
        kaug_ref[:, (2 * g) * LANE:(2 * g + 1) * LANE] = jnp.where(lane < MOBA_DIM, ch, onehot).astype(BF16)
        chs = pltpu.roll(ch, MOBA_DIM, axis=1)
        kaug_ref[:, (2 * g + 1) * LANE:(2 * g + 2) * LANE] = jnp.where(lane < MOBA_DIM, chs, onehot).astype(BF16)


def _attn_proj(x, w, tabs, seq_rows):
    n = x.shape[0]
    tm = ROW_TILE
    assert tm == MOBA_BLOCK and n % tm == 0 and seq_rows % tm == 0
    tiles_per_seq = seq_rows // tm
    nsub = tm // LANE
    c64, s64, c32, s32 = tabs
    row = lambda width: pl.BlockSpec((tm, width), lambda i: (i, 0))
    tab = pl.BlockSpec((tm, LANE), lambda i: (i % tiles_per_seq, 0))
    full = lambda a: pl.BlockSpec(a.shape, lambda i: (0,) * a.ndim)
    outs = pl.pallas_call(
        functools.partial(_proj_kernel, blocks_per_seq=tiles_per_seq),
        out_shape=[
            jax.ShapeDtypeStruct((n // LANE, MLA_HEADS, LANE, 2 * LANE), BF16),
            jax.ShapeDtypeStruct((n, 2 * LANE), BF16),
            jax.ShapeDtypeStruct((n, KV_RANK), F32),
            jax.ShapeDtypeStruct((n, ROPE_DIM), F32),
            jax.ShapeDtypeStruct((n, MOBA_WIDTH), F32),
            jax.ShapeDtypeStruct((n, MOBA_WIDTH), F32),
            jax.ShapeDtypeStruct((n, MOBA_HEADS * LANE), BF16),
            jax.ShapeDtypeStruct((n, MOBA_WIDTH), F32),
            jax.ShapeDtypeStruct((n, MOBA_WIDTH), BF16),
            jax.ShapeDtypeStruct((n // tm, 1, MOBA_WIDTH), F32),
        ],
        grid=(n // tm,),
        in_specs=[row(D_MODEL), full(w["w_in"]), full(w["q_norm"]), full(w["w_q_b"]), full(w["wk"]),
                  full(w["kv_norm"]), tab, tab, tab, tab],
        out_specs=[
            pl.BlockSpec((nsub, MLA_HEADS, LANE, 2 * LANE), lambda i: (i, 0, 0, 0)),
            row(2 * LANE), row(KV_RANK), row(ROPE_DIM), row(MOBA_WIDTH), row(MOBA_WIDTH),
            row(MOBA_HEADS * LANE), row(MOBA_WIDTH), row(MOBA_WIDTH),
            pl.BlockSpec((1, 1, MOBA_WIDTH), lambda i: (i, 0, 0)),
        ],
        compiler_params=_cparams(("arbitrary",)),
        name="attn_proj",
    )(x, w["w_in"], w["q_norm"], w["w_q_b"], w["wk"], w["kv_norm"], c64, s64, c32, s32)
    return outs


def _mla_prompt_kernel(q_ref, kcat_ref, o_ref, m_ref, l_ref, acc_ref):
    qi = pl.program_id(1)
    rows = MLA_HEADS * MLA_TQ
    q = q_ref[0].reshape(rows, 2 * LANE)
    cexp = MLA_SCALE * LOG2E

    def scores(kt):
        k = kcat_ref[pl.ds(pl.multiple_of(kt * MLA_TK, MLA_TK), MLA_TK), :]
        return _bdot_nt(q, k), k[:, :KV_RANK]

    kd = (qi * MLA_TQ) // MLA_TK
    s, v = scores(kd)
    col = kd * MLA_TK + lax.broadcasted_iota(jnp.int32, (rows, MLA_TK), 1)
    rowpos = qi * MLA_TQ + lax.broadcasted_iota(jnp.int32, (rows, MLA_TK), 0) % MLA_TQ
    s = jnp.where(col <= rowpos, s, NEG)
    m = jnp.max(s, axis=1, keepdims=True)
    p = jnp.exp2((s - m) * cexp)
    m_ref[...] = m
    l_ref[...] = jnp.sum(p, axis=1, keepdims=True)
    acc_ref[...] = jnp.dot(p.astype(BF16), v, preferred_element_type=F32)

    def body(kt, carry):
        s, v = scores(kt)
        m_old = m_ref[...]
        m_new = jnp.maximum(m_old, jnp.max(s, axis=1, keepdims=True))
        alpha = jnp.exp2((m_old - m_new) * cexp)
        p = jnp.exp2((s - m_new) * cexp)
        m_ref[...] = m_new
        l_ref[...] = l_ref[...] * alpha + jnp.sum(p, axis=1, keepdims=True)
        acc_ref[...] = acc_ref[...] * alpha + jnp.dot(p.astype(BF16), v, preferred_element_type=F32)
        return carry

    lax.fori_loop(0, kd, body, 0)
    o = (acc_ref[...] / l_ref[...]).astype(BF16)
    o_ref[0] = o.reshape(MLA_HEADS, MLA_TQ, KV_RANK)


def _mla_prompt(qcat, kcat, batch, seq):
    assert MLA_TQ == LANE and seq % MLA_TK == 0 and MLA_TK % MLA_TQ == 0
    nq = seq // MLA_TQ
    rows = MLA_HEADS * MLA_TQ
    return pl.pallas_call(
        _mla_prompt_kernel,
        out_shape=jax.ShapeDtypeStruct((batch * nq, MLA_HEADS, MLA_TQ, KV_RANK), BF16),
        grid=(batch, nq),
        in_specs=[pl.BlockSpec((1, MLA_HEADS, MLA_TQ, 2 * LANE), lambda b, i: (b * nq + i, 0, 0, 0)),
                  pl.BlockSpec((seq, 2 * LANE), lambda b, i: (b, 0))],
        out_specs=pl.BlockSpec((1, MLA_HEADS, MLA_TQ, KV_RANK), lambda b, i: (b * nq + i, 0, 0, 0)),
        scratch_shapes=[pltpu.VMEM((rows, 1), F32), pltpu.VMEM((rows, 1), F32), pltpu.VMEM((rows, KV_RANK), F32)],
        compiler_params=_cparams(("arbitrary", "arbitrary")),
        name="mla_prompt",
    )(qcat, kcat)


def _class_reduce(v, op):
    for sh in (8, 16, 32, 64):
        v = op(v, pltpu.roll(v, sh, axis=1))
    return v


def _moba_select_kernel(qm_ref, kmbd_ref, perm_ref, qaug_ref, *, blocks_per_seq):
    tm = qm_ref.shape[0]
    qm = qm_ref[...]
    gate = jnp.dot(qm, kmbd_ref[...], precision=lax.Precision.HIGHEST, preferred_element_type=F32)
    lane = lax.broadcasted_iota(jnp.int32, (tm, LANE), 1)
    nblk = lane // MOBA_HEADS
    own = pl.program_id(0) % blocks_per_seq
    past = nblk < own
    g = jnp.where(past, gate, -jnp.inf)
    picked = jnp.zeros((tm, LANE), F32)
    for _ in range(MOBA_TOPK):
        top = _class_reduce(g, jnp.maximum)
        first = _class_reduce(jnp.where(g == top, nblk, LANE), jnp.minimum)
        hit = nblk == first
        picked = picked + jnp.where(hit & (top > -jnp.inf), 1.0, 0.0)
        g = jnp.where(hit, -jnp.inf, g)
    masked = jnp.where(past & (picked > 0.5), 0.0, 1.0)
    bias = jnp.dot(masked.astype(BF16), perm_ref[...], preferred_element_type=F32) * NEG
    for gp in range(MOBA_HEADS // 2):
        ch = qm[:, gp * LANE:(gp + 1) * LANE]
        for e, src in ((0, ch), (1, pltpu.roll(ch, MOBA_DIM, axis=1))):
            hd = 2 * gp + e
            slot = jnp.where(lane < MOBA_DIM, src, 0.0) + bias[:, hd * LANE:(hd + 1) * LANE]
            qaug_ref[:, hd * LANE:(hd + 1) * LANE] = slot.astype(BF16)


def _moba_select(qm, kmean, batch, seq):
    n = qm.shape[0]
    tm = ROW_TILE
    nblk = seq // MOBA_BLOCK
    assert tm == MOBA_BLOCK and nblk * MOBA_HEADS <= LANE
    km4 = kmean.reshape(batch, nblk, MOBA_HEADS, MOBA_DIM)
    eye = jnp.eye(MOBA_HEADS, dtype=F32)
    kmbd = jnp.einsum("bnhd,hk->bhdnk", km4, eye).reshape(batch, MOBA_WIDTH, nblk * MOBA_HEADS)
    kmbd = jnp.pad(kmbd, ((0, 0), (0, 0), (0, LANE - nblk * MOBA_HEADS)))
    src = jnp.arange(LANE)
    dst = (src % MOBA_HEADS) * LANE + MOBA_DIM + src // MOBA_HEADS
    perm = (jnp.arange(MOBA_HEADS * LANE)[None, :] == dst[:, None]).astype(BF16)
    return pl.pallas_call(
        functools.partial(_moba_select_kernel, blocks_per_seq=nblk),
        out_shape=jax.ShapeDtypeStruct((n, MOBA_HEADS * LANE), BF16),
        grid=(n // tm,),
        in_specs=[pl.BlockSpec((tm, MOBA_WIDTH), lambda i: (i, 0)),
                  pl.BlockSpec((None, MOBA_WIDTH, LANE), lambda i: (i // nblk, 0, 0)),
                  pl.BlockSpec(perm.shape, lambda i: (0, 0))],
        out_specs=pl.BlockSpec((tm, MOBA_HEADS * LANE), lambda i: (i, 0)),
        compiler_params=_cparams(("arbitrary",)),
        name="moba_select",
    )(qm, kmbd, perm)


def _moba_prompt_kernel(qaug_ref, kaug_ref, v_ref, o_ref, m_ref, l_ref, acc_ref):
    own = pl.program_id(1)
    tq = qaug_ref.shape[0]
    cexp = MOBA_SCALE * LOG2E
    lane = lax.broadcasted_iota(jnp.int32, (tq, LANE), 1)

    def scores(r0, rows, hd, with_bias):
        q = qaug_ref[:, hd * LANE:(hd + 1) * LANE]
        if not with_bias:
            q = jnp.where(lane < MOBA_DIM, q.astype(F32), 0.0).astype(BF16)
        k = kaug_ref[pl.ds(r0, rows), hd * LANE:(hd + 1) * LANE]
        v = v_ref[pl.ds(r0, rows), (hd // 2) * LANE:(hd // 2 + 1) * LANE]
        return _bdot_nt(q, k), v

    tri = (lax.broadcasted_iota(jnp.int32, (tq, MOBA_BLOCK), 1)
           <= lax.broadcasted_iota(jnp.int32, (tq, MOBA_BLOCK), 0))
    for hd in range(MOBA_HEADS):
        s, v = scores(pl.multiple_of(own * MOBA_BLOCK, MOBA_BLOCK), MOBA_BLOCK, hd, False)
        s = jnp.where(tri, s, NEG)
        m = jnp.max(s, axis=1, keepdims=True)
        p = jnp.exp2((s - m) * cexp)
        m_ref[hd] = m
        l_ref[hd] = jnp.sum(p, axis=1, keepdims=True)
        acc_ref[hd] = jnp.dot(p.astype(BF16), v, preferred_element_type=F32)

    group_rows = MOBA_GROUP * MOBA_BLOCK

    def body(gi, carry):
        for hd in range(MOBA_HEADS):
            s, v = scores(pl.multiple_of(gi * group_rows, group_rows), group_rows, hd, True)
            m_old = m_ref[hd]
            m_new = jnp.maximum(m_old, jnp.max(s, axis=1, keepdims=True))
            alpha = jnp.exp2((m_old - m_new) * cexp)
            p = jnp.exp2((s - m_new) * cexp)
            m_ref[hd] = m_new
            l_ref[hd] = l_ref[hd] * alpha + jnp.sum(p, axis=1, keepdims=True)
            acc_ref[hd] = acc_ref[hd] * alpha + jnp.dot(p.astype(BF16), v, preferred_element_type=F32)
        return carry

    lax.fori_loop(0, (own + MOBA_GROUP - 1) // MOBA_GROUP, body, 0)
    for gp in range(MOBA_HEADS // 2):
        lo = acc_ref[2 * gp] / l_ref[2 * gp]
        hi = acc_ref[2 * gp + 1] / l_ref[2 * gp + 1]
        o_ref[:, gp * LANE:(gp + 1) * LANE] = jnp.where(lane < MOBA_DIM, lo, hi).astype(BF16)


def _moba_prompt(qaug, kaug, vmb, batch, seq):
    tq = MOBA_BLOCK
    nq = seq // tq
    assert seq % (MOBA_GROUP * MOBA_BLOCK) == 0
    return pl.pallas_call(
        _moba_prompt_kernel,
        out_shape=jax.ShapeDtypeStruct((batch * seq, MOBA_WIDTH), BF16),
        grid=(batch, nq),
        in_specs=[pl.BlockSpec((tq, MOBA_HEADS * LANE), lambda b, i: (b * nq + i, 0)),
                  pl.BlockSpec((seq, MOBA_HEADS * LANE), lambda b, i: (b, 0)),
                  pl.BlockSpec((seq, MOBA_WIDTH), lambda b, i: (b, 0))],
        out_specs=pl.BlockSpec((tq, MOBA_WIDTH), lambda b, i: (b * nq + i, 0)),
        scratch_shapes=[pltpu.VMEM((MOBA_HEADS, tq, 1), F32), pltpu.VMEM((MOBA_HEADS, tq, 1), F32),
                        pltpu.VMEM((MOBA_HEADS, tq, LANE), F32)],
        compiler_params=_cparams(("arbitrary", "arbitrary")),
        name="moba_prompt",
    )(qaug, kaug, vmb)


def _attn_out_kernel(x_ref, omla_ref, omoba_ref, wv_ref, wout_ref, g_ref, b_ref, y_ref):
    tm = x_ref.shape[0]
    x = x_ref[...]
    mix = _bdot(omoba_ref[...], wout_ref[MLA_HEADS * V_DIM:, :])
    for gp in range(MLA_HEADS // 2):
        v2 = jnp.zeros((tm, LANE), F32)
        for e in range(2):
            hd = 2 * gp + e
            o = omla_ref[:, hd].reshape(tm, KV_RANK)
            v2 = v2 + jnp.dot(o, wv_ref[hd], preferred_element_type=F32)
        mix = mix + _bdot(v2, wout_ref[gp * LANE:(gp + 1) * LANE, :])
    y_ref[...] = _layer_norm(ALPHA * x + mix, g_ref[...], b_ref[...])


def _attn_out(x, omla, omoba, w, g, b):
    n = x.shape[0]
    tm = ROW_TILE
    nsub = tm // LANE
    full = lambda a: pl.BlockSpec(a.shape, lambda i: (0,) * a.ndim)
    return pl.pallas_call(
        _attn_out_kernel,
        out_shape=jax.ShapeDtypeStruct((n, D_MODEL), F32),
        grid=(n // tm,),
        in_specs=[pl.BlockSpec((tm, D_MODEL), lambda i: (i, 0)),
                  pl.BlockSpec((nsub, MLA_HEADS, LANE, KV_RANK), lambda i: (i, 0, 0, 0)),
                  pl.BlockSpec((tm, MOBA_WIDTH), lambda i: (i, 0)),
                  full(w["wv"]), full(w["w_out"]), full(g), full(b)],
        out_specs=pl.BlockSpec((tm, D_MODEL), lambda i: (i, 0)),
        compiler_params=_cparams(("arbitrary",)),
        name="attn_out",
    )(x, omla, omoba, w["wv"], w["w_out"], g, b)


def _page_copies(pt_ref, pool_ref, buf_ref, sem_ref, layer, first_page, slot):
    return [pltpu.make_async_copy(pool_ref.at[layer, pt_ref[first_page + i]], buf_ref.at[slot, i], sem_ref.at[slot])
            for i in range(PAGES_PER_STEP)]


def _mla_sample_kernel(pt_ref, qlat_ref, qrope_ref, latn_ref, ropen_ref, lat_hbm, rope_hbm,
                       o_ref, latbuf, ropebuf, latsem, ropesem, m_ref, l_ref, acc_ref, *, layer, n_pages):
    pps = PAGES_PER_STEP
    j = pl.program_id(1)
    steps = n_pages // pps
    g = pl.program_id(0) * steps + j
    total = pl.num_programs(0) * steps
    slot = g % PAGE_SLOTS
    cexp = MLA_SCALE * LOG2E
    ql = qlat_ref[0]
    qr = qrope_ref[0]
    rows = ql.shape[0]

    def copies(step, slot_):
        first = (step // steps) * n_pages + (step % steps) * pps
        return (_page_copies(pt_ref, lat_hbm, latbuf, latsem, layer, first, slot_)
                + _page_copies(pt_ref, rope_hbm, ropebuf, ropesem, layer, first, slot_))

    def start(step):
        @pl.when(step < total)
        def _():
            for c in copies(step, step % PAGE_SLOTS):
                c.start()

    @pl.when(g == 0)
    def _():
        for first_step in range(PAGE_SLOTS - 1):
            start(first_step)

    start(g + PAGE_SLOTS - 1)
    for c in copies(g, slot):
        c.wait()

    @pl.when(j == 0)
    def _():
        m_ref[...] = jnp.full((rows, 1), NEG, F32)
        l_ref[...] = jnp.zeros((rows, 1), F32)
        acc_ref[...] = jnp.zeros((rows, KV_RANK), F32)

    lat = jnp.concatenate([latbuf[slot, i] for i in range(pps)], axis=0).astype(BF16)
    rpt = jnp.concatenate([ropebuf[slot, i] for i in range(pps)], axis=1)
    s = _bdot_nt(ql, lat) + _bdot(qr, rpt)
    m_old = m_ref[...]
    m_new = jnp.maximum(m_old, jnp.max(s, axis=1, keepdims=True))
    alpha = jnp.exp2((m_old - m_new) * cexp)
    p = jnp.exp2((s - m_new) * cexp)
    m_ref[...] = m_new
    l_ref[...] = l_ref[...] * alpha + jnp.sum(p, axis=1, keepdims=True)
    acc_ref[...] = acc_ref[...] * alpha + jnp.dot(p.astype(BF16), lat, preferred_element_type=F32)

    @pl.when(j == pl.num_programs(1) - 1)
    def _():
        latn = latn_ref[0].astype(BF16)
        s = _bdot_nt(ql, latn) + _bdot_nt(qr, ropen_ref[0])
        tq = lax.broadcasted_iota(jnp.int32, s.shape, 0) // MLA_HEADS
        tk = lax.broadcasted_iota(jnp.int32, s.shape, 1)
        s = jnp.where(tk <= tq, s, NEG)
        m_old = m_ref[...]
        m_new = jnp.maximum(m_old, jnp.max(s, axis=1, keepdims=True))
        alpha = jnp.exp2((m_old - m_new) * cexp)
        p = jnp.exp2((s - m_new) * cexp)
        l = l_ref[...] * alpha + jnp.sum(p, axis=1, keepdims=True)
        acc = acc_ref[...] * alpha + jnp.dot(p.astype(BF16), latn, preferred_element_type=F32)
        o_ref[0] = (acc / l).astype(BF16)


def _mla_sample(page_table, qlat, qrope, lat_new, rope_new, pool_lat, pool_rope_t, layer):
    nb, rows, _ = qlat.shape
    n_new = lat_new.shape[1]
    n_pages = page_table.shape[1]
    pps = PAGES_PER_STEP
    assert n_pages % pps == 0
    pt = page_table.reshape(-1)

    per_seq = lambda a: pl.BlockSpec((1,) + a.shape[1:], lambda b, j, pt_ref: (b, 0, 0))
    hbm = pl.BlockSpec(memory_space=pl.ANY)
    grid_spec = pltpu.PrefetchScalarGridSpec(
        num_scalar_prefetch=1,
        grid=(nb, n_pages // pps),
        in_specs=[per_seq(qlat), per_seq(qrope), per_seq(lat_new), per_seq(rope_new), hbm, hbm],
        out_specs=pl.BlockSpec((1, rows, KV_RANK), lambda b, j, pt_ref: (b, 0, 0)),
        scratch_shapes=[pltpu.VMEM((PAGE_SLOTS, pps, PAGE_SIZE, KV_RANK), F32),
                        pltpu.VMEM((PAGE_SLOTS, pps, ROPE_DIM, PAGE_SIZE), F32),
                        pltpu.SemaphoreType.DMA((PAGE_SLOTS,)), pltpu.SemaphoreType.DMA((PAGE_SLOTS,)),
                        pltpu.VMEM((rows, 1), F32), pltpu.VMEM((rows, 1), F32), pltpu.VMEM((rows, KV_RANK), F32)],
    )
    return pl.pallas_call(
        functools.partial(_mla_sample_kernel, layer=layer, n_pages=n_pages),
        out_shape=jax.ShapeDtypeStruct((nb, rows, KV_RANK), BF16),
        grid_spec=grid_spec,
        compiler_params=_cparams(("arbitrary", "arbitrary")),
        name="mla_sample",
    )(pt, qlat, qrope, lat_new, rope_new, pool_lat, pool_rope_t)


def _moba_sample_kernel(pt_ref, qbd_ref, qbdf_ref, kn_ref, vn_ref, expand_ref, kt_hbm, vt_hbm,
                        o_ref, buf, sem, s_ref, ksum_ref, p_ref, pn_ref, den_ref, acc_ref,
                        *, n_new, n_past_blocks, layer, n_pages):
    pps = PAGES_PER_STEP
    phase = pl.program_id(1)
    j = pl.program_id(2)
    steps = n_pages // pps
    g = (pl.program_id(0) * 2 + phase) * steps + j
    total = pl.num_programs(0) * 2 * steps
    slot = g % PAGE_SLOTS
    cexp = MOBA_SCALE * LOG2E

    def copies(pool_ref, step):
        first = (step // (2 * steps)) * n_pages + (step % steps) * pps
        return _page_copies(pt_ref, pool_ref, buf, sem, layer, first, step % PAGE_SLOTS)

    def start(step):
        is_v = (step // steps) % 2

        @pl.when((step < total) & (is_v == 0))
        def _():
            for c in copies(kt_hbm, step):
                c.start()

        @pl.when((step < total) & (is_v == 1))
        def _():
            for c in copies(vt_hbm, step):
                c.start()

    @pl.when(g == 0)
    def _():
        for first_step in range(PAGE_SLOTS - 1):
            start(first_step)

    start(g + PAGE_SLOTS - 1)
    for c in copies(kt_hbm, g):
        c.wait()
    qbd = qbd_ref[0]
    pages_per_block = MOBA_BLOCK // PAGE_SIZE

    @pl.when((phase == 0) & (j == 0))
    def _():
        ksum_ref[...] = jnp.zeros(ksum_ref.shape, F32)

    @pl.when(phase == 0)
    def _():
        lane = lax.broadcasted_iota(jnp.int32, ksum_ref.shape, 1)
        for i0 in range(0, pps, pages_per_block):
            ksum = None
            for i in range(i0, i0 + pages_per_block):
                kt = buf[slot, i]
                col = pl.multiple_of((j * pps + i) * PAGE_SIZE, PAGE_SIZE)
                s_ref[:, pl.ds(col, PAGE_SIZE)] = _bdot(qbd, kt)
                ksum = kt if ksum is None else ksum + kt
            blk = (j * pps + i0) // pages_per_block
            ksum_ref[...] = jnp.where(lane == blk, jnp.sum(ksum, axis=1, keepdims=True), ksum_ref[...])

    @pl.when((phase == 1) & (j == 0))
    def _():
        kmean = ksum_ref[...] * (1.0 / MOBA_BLOCK)
        gate = jnp.dot(qbdf_ref[0], kmean, precision=lax.Precision.HIGHEST, preferred_element_type=F32)
        nidx = lax.broadcasted_iota(jnp.int32, gate.shape, 1)
        g = jnp.where(nidx < n_past_blocks, gate, -jnp.inf)
        nidx = nidx.astype(F32)
        picked = jnp.zeros(gate.shape, F32)
        for _ in range(MOBA_TOPK):
            top = jnp.max(g, axis=1, keepdims=True)
            first = jnp.min(jnp.where(g == top, nidx, float(LANE)), axis=1, keepdims=True)
            hit = nidx == first
            picked = picked + jnp.where(hit & (top > -jnp.inf), 1.0, 0.0)
            g = jnp.where(hit, -jnp.inf, g)
        chosen = jnp.dot(picked.astype(BF16), expand_ref[...], preferred_element_type=F32)
        s = jnp.where(chosen > 0.5, s_ref[...], NEG)
        sn = _bdot_nt(qbd, kn_ref[0])
        tq = lax.broadcasted_iota(jnp.int32, sn.shape, 0) // MOBA_HEADS
        tk = lax.broadcasted_iota(jnp.int32, sn.shape, 1)
        sn = jnp.where(tk <= tq, sn, NEG)
        m = jnp.maximum(jnp.max(s, axis=1, keepdims=True), jnp.max(sn, axis=1, keepdims=True))
        p = jnp.exp2((s - m) * cexp)
        pn = jnp.exp2((sn - m) * cexp)
        den_ref[...] = jnp.sum(p, axis=1, keepdims=True) + jnp.sum(pn, axis=1, keepdims=True)
        p_ref[...] = p.astype(BF16)
        pn_ref[...] = pn
        acc_ref[...] = jnp.zeros(acc_ref.shape, F32)

    @pl.when(phase == 1)
    def _():
        acc = acc_ref[...]
        for i in range(pps):
            col = pl.multiple_of((j * pps + i) * PAGE_SIZE, PAGE_SIZE)
            acc = acc + _bdot_nt(p_ref[:, pl.ds(col, PAGE_SIZE)], buf[slot, i])
        acc_ref[...] = acc

    @pl.when((phase == 1) & (j == pl.num_programs(2) - 1))
    def _():
        o = (acc_ref[...] + _bdot(pn_ref[...], vn_ref[0])) / den_ref[...]
        hrow = lax.broadcasted_iota(jnp.int32, o.shape, 0) % MOBA_HEADS
        hcol = lax.broadcasted_iota(jnp.int32, o.shape, 1) // MOBA_DIM
        o = jnp.where(hrow == hcol, o, 0.0)
        for t in range(n_new):
            o_ref[0, t:t + 1, :] = jnp.sum(o[t * MOBA_HEADS:(t + 1) * MOBA_HEADS], axis=0,
                                           keepdims=True).astype(BF16)


def _moba_sample(page_table, qbd, qbdf, k_new, v_new, pool_kt, pool_vt, layer):
    nb, rows, _ = qbd.shape
    n_new = k_new.shape[1]
    n_pages = page_table.shape[1]
    pps = PAGES_PER_STEP
    past_len = n_pages * PAGE_SIZE
    pages_per_block = MOBA_BLOCK // PAGE_SIZE
    assert n_pages % pps == 0 and pps % pages_per_block == 0 and past_len % MOBA_BLOCK == 0 and n_new <= MOBA_BLOCK
    n_past_blocks = past_len // MOBA_BLOCK
    assert n_past_blocks <= LANE
    steps = n_pages // pps
    pt = page_table.reshape(-1)
    expand = (jnp.arange(past_len)[None, :] // MOBA_BLOCK == jnp.arange(LANE)[:, None]).astype(BF16)

    per_seq = lambda a: pl.BlockSpec((1,) + a.shape[1:], lambda b, ph, j, pt_ref: (b, 0, 0))
    hbm = pl.BlockSpec(memory_space=pl.ANY)
    grid_spec = pltpu.PrefetchScalarGridSpec(
        num_scalar_prefetch=1,
        grid=(nb, 2, steps),
        in_specs=[per_seq(qbd), per_seq(qbdf), per_seq(k_new), per_seq(v_new),
                  pl.BlockSpec(expand.shape, lambda b, ph, j, pt_ref: (0, 0)), hbm, hbm],
        out_specs=pl.BlockSpec((1, n_new, MOBA_WIDTH), lambda b, ph, j, pt_ref: (b, 0, 0)),
        scratch_shapes=[pltpu.VMEM((PAGE_SLOTS, pps, MOBA_WIDTH, PAGE_SIZE), F32),
                        pltpu.SemaphoreType.DMA((PAGE_SLOTS,)),
                        pltpu.VMEM((rows, past_len), F32),
                        pltpu.VMEM((MOBA_WIDTH, LANE), F32),
                        pltpu.VMEM((rows, past_len), BF16),
                        pltpu.VMEM((rows, n_new), F32),
                        pltpu.VMEM((rows, 1), F32),
                        pltpu.VMEM((rows, MOBA_WIDTH), F32)],
    )
    return pl.pallas_call(
        functools.partial(_moba_sample_kernel, n_new=n_new, n_past_blocks=n_past_blocks, layer=layer,
                          n_pages=n_pages),
        out_shape=jax.ShapeDtypeStruct((nb, n_new, MOBA_WIDTH), BF16),
        grid_spec=grid_spec,
        compiler_params=_cparams(("arbitrary", "arbitrary", "arbitrary")),
        name="moba_sample",
    )(pt, qbd, qbdf, k_new, v_new, expand, pool_kt, pool_vt)


def _s5_param_kernel(are_c_ref, aim_c_ref, dt_c_ref, bre_ref, bim_ref, are_r_ref, aim_r_ref, dt_r_ref,
                     bbre_ref, bbim_ref, pw_ref, step_ref):
    def abar(are, aim, dt):
        mag = jnp.exp(are * dt)
        return mag * jnp.cos(aim * dt), mag * jnp.sin(aim * dt)

    are, aim = are_c_ref[...], aim_c_ref[...]
    ar, ai = abar(are, aim, dt_c_ref[...])
    nr, ni = ar - 1.0, ai
    den = are * are + aim * aim
    cr = (nr * are + ni * aim) / den
    ci = (ni * are - nr * aim) / den
    bre, bim = bre_ref[...], bim_ref[...]
    bbre_ref[...] = cr * bre - ci * bim
    bbim_ref[...] = cr * bim + ci * bre
    ar, ai = abar(are_r_ref[...], aim_r_ref[...], dt_r_ref[...])
    pr, pi = ar, ai
    powers = []
    for i in range(SUBLANE):
        pw_ref[0, i:i + 1, :] = pr
        pw_ref[1, i:i + 1, :] = pi
        powers.append((pr, pi))
        pr, pi = pr * ar - pi * ai, pr * ai + pi * ar
    zero = jnp.zeros_like(ar)
    for k in range(3):
        d = 1 << k
        for i in range(SUBLANE):
            step_ref[2 * k, i:i + 1, :] = powers[d - 1][0] if i >= d else zero
            step_ref[2 * k + 1, i:i + 1, :] = powers[d - 1][1] if i >= d else zero


def _s5_params(a_re, a_im, log_dt, b_re, b_im, c_re, c_im):
    col = lambda a: a.reshape(SSM_CH, 1)
    rowv = lambda a: a.reshape(1, SSM_CH)
    dt = jnp.exp(log_dt.astype(F32))
    dt_full = jnp.broadcast_to(dt[:, None], (SSM_GROUPS, SSM_STATE))
    vm = lambda shape: pl.BlockSpec(shape, lambda: (0,) * len(shape))
    ins = [col(a_re), col(a_im), col(dt_full), b_re.reshape(SSM_CH, SSM_GROUP), b_im.reshape(SSM_CH, SSM_GROUP),
           rowv(a_re), rowv(a_im), rowv(dt_full)]
    bbre, bbim, pw, step = pl.pallas_call(
        _s5_param_kernel,
        out_shape=[jax.ShapeDtypeStruct((SSM_CH, SSM_GROUP), F32), jax.ShapeDtypeStruct((SSM_CH, SSM_GROUP), F32),
                   jax.ShapeDtypeStruct((2, SUBLANE, SSM_CH), F32), jax.ShapeDtypeStruct((6, SUBLANE, SSM_CH), F32)],
        in_specs=[vm(a.shape) for a in ins],
        out_specs=[vm((SSM_CH, SSM_GROUP)), vm((SSM_CH, SSM_GROUP)), vm((2, SUBLANE, SSM_CH)), vm((6, SUBLANE, SSM_CH))],
        compiler_params=pltpu.CompilerParams(vmem_limit_bytes=VMEM_LIMIT),
        name="s5_params",
    )(*ins)
    gpc = SSM_GROUPS // S5_CHUNKS
    eye = jnp.eye(gpc, dtype=F32)

    def in_mat(bb):
        b4 = bb.reshape(S5_CHUNKS, gpc, SSM_STATE, SSM_GROUP)
        return jnp.einsum("kgpc,gh->kgchp", b4, eye).reshape(S5_CHUNKS, S5_CIN, S5_CST)

    def out_mat(cc):
        c4 = cc.reshape(S5_CHUNKS, gpc, SSM_GROUP, SSM_STATE)
        return jnp.einsum("kgcp,gh->kgphc", c4, eye).reshape(S5_CHUNKS, S5_CST, S5_CIN)

    wb = jnp.concatenate([in_mat(bbre), in_mat(bbim)], axis=-1).astype(BF16)
    return dict(wb=wb, wc_re=out_mat(c_re.astype(F32)).astype(BF16), wc_im=out_mat(c_im.astype(F32)).astype(BF16),
                pw=pw, step=step)


def _s5_tail(x, y, d_ref, wglu_ref, g_ref, b_ref):
    z = _gelu(y + d_ref[...] * x)
    zz = _bdot(z, wglu_ref[...])
    out = zz[:, :D_MODEL] * _sigmoid(zz[:, D_MODEL:])
    return _layer_norm(ALPHA * x + out, g_ref[...], b_ref[...])


def _s5_prompt_kernel(x_ref, wb_ref, wcre_ref, wcim_ref, pw_ref, step_ref, d_ref, wglu_ref, g_ref, b_ref,
                      y_ref, hl_ref, carry_ref, h_scr):
    tt = x_ref.shape[0]

    @pl.when(pl.program_id(1) == 0)
    def _():
        carry_ref[...] = jnp.zeros(carry_ref.shape, F32)

    x = x_ref[...]
    ys = []
    for k in range(S5_CHUNKS):
        lo = k * S5_CST
        h_scr[...] = _bdot(x[:, k * S5_CIN:(k + 1) * S5_CIN], wb_ref[k])
        pr, pi = pw_ref[0, :, lo:lo + S5_CST], pw_ref[1, :, lo:lo + S5_CST]
        steps = [(step_ref[2 * s, :, lo:lo + S5_CST], step_ref[2 * s + 1, :, lo:lo + S5_CST], 1 << s)
                 for s in range(3)]

        def block(i, carry, pr=pr, pi=pi, steps=steps):
            cr, ci = carry
            r0 = pl.multiple_of(i * SUBLANE, SUBLANE)
            hr = h_scr[pl.ds(r0, SUBLANE), 0:S5_CST]
            hi = h_scr[pl.ds(r0, SUBLANE), S5_CST:2 * S5_CST]
            for sr, si, sh in steps:
                rr, ri = pltpu.roll(hr, sh, axis=0), pltpu.roll(hi, sh, axis=0)
                hr, hi = hr + sr * rr - si * ri, hi + sr * ri + si * rr
            hr, hi = hr + pr * cr - pi * ci, hi + pr * ci + pi * cr
            h_scr[pl.ds(r0, SUBLANE), 0:S5_CST] = hr
            h_scr[pl.ds(r0, SUBLANE), S5_CST:2 * S5_CST] = hi
            return hr[SUBLANE - 1:SUBLANE], hi[SUBLANE - 1:SUBLANE]

        cr, ci = lax.fori_loop(0, tt // SUBLANE, block,
                               (carry_ref[k, 0:1, 0:S5_CST], carry_ref[k, 0:1, S5_CST:2 * S5_CST]))
        carry_ref[k, 0:1, 0:S5_CST] = cr
        carry_ref[k, 0:1, S5_CST:2 * S5_CST] = ci
        ys.append(_bdot(h_scr[:, 0:S5_CST], wcre_ref[k]) - _bdot(h_scr[:, S5_CST:2 * S5_CST], wcim_ref[k]))
    y_ref[...] = _s5_tail(x, jnp.concatenate(ys, axis=-1), d_ref, wglu_ref, g_ref, b_ref)
    hl_ref[0] = carry_ref[...]


def _s5_prompt(x, sp, d, wglu, g, b, batch, seq):
    tt = ROW_TILE
    nt = seq // tt
    full = lambda a: pl.BlockSpec(a.shape, lambda bb, i: (0,) * a.ndim)
    y, hl = pl.pallas_call(
        _s5_prompt_kernel,
        out_shape=[jax.ShapeDtypeStruct((batch * seq, D_MODEL), F32),
                   jax.ShapeDtypeStruct((batch, S5_CHUNKS, SUBLANE, 2 * S5_CST), F32)],
        grid=(batch, nt),
        in_specs=[pl.BlockSpec((tt, D_MODEL), lambda bb, i: (bb * nt + i, 0)),
                  full(sp["wb"]), full(sp["wc_re"]), full(sp["wc_im"]), full(sp["pw"]), full(sp["step"]),
                  full(d), full(wglu), full(g), full(b)],
        out_specs=[pl.BlockSpec((tt, D_MODEL), lambda bb, i: (bb * nt + i, 0)),
                   pl.BlockSpec((1, S5_CHUNKS, SUBLANE, 2 * S5_CST), lambda bb, i: (bb, 0, 0, 0))],
        scratch_shapes=[pltpu.VMEM((S5_CHUNKS, SUBLANE, 2 * S5_CST), F32), pltpu.VMEM((tt, 2 * S5_CST), F32)],
        compiler_params=_cparams(("arbitrary", "arbitrary")),
        name="s5_prompt",
    )(x, sp["wb"], sp["wc_re"], sp["wc_im"], sp["pw"], sp["step"], d, wglu, g, b)
    h_re = hl[:, :, 0, :S5_CST].reshape(batch, SSM_GROUPS, SSM_STATE)
    h_im = hl[:, :, 0, S5_CST:].reshape(batch, SSM_GROUPS, SSM_STATE)
    return y, h_re, h_im


def _s5_sample_kernel(x_ref, h0re_ref, h0im_ref, wb_ref, wcre_ref, wcim_ref, pw_ref, d_ref, wglu_ref, g_ref, b_ref,
                      y_ref, hre_ref, him_ref):
    @pl.when(pl.program_id(0) == 0)
    def _():
        hre_ref[...] = h0re_ref[...]
        him_ref[...] = h0im_ref[...]

    x = x_ref[...]
    ys = []
    for k in range(S5_CHUNKS):
        lo = k * S5_CST
        bu = _bdot(x[:, k * S5_CIN:(k + 1) * S5_CIN], wb_ref[k])
        ar, ai = pw_ref[0, 0:1, lo:lo + S5_CST], pw_ref[1, 0:1, lo:lo + S5_CST]
        hr, hi = hre_ref[:, lo:lo + S5_CST], him_ref[:, lo:lo + S5_CST]
        nr = ar * hr - ai * hi + bu[:, :S5_CST]
        ni = ar * hi + ai * hr + bu[:, S5_CST:]
        hre_ref[:, lo:lo + S5_CST] = nr
        him_ref[:, lo:lo + S5_CST] = ni
        ys.append(_bdot(nr, wcre_ref[k]) - _bdot(ni, wcim_ref[k]))
    y_ref[...] = _s5_tail(x, jnp.concatenate(ys, axis=-1), d_ref, wglu_ref, g_ref, b_ref)


def _s5_sample(x, h0_re, h0_im, sp, d, wglu, g, b, nb, n_new):
    full = lambda a: pl.BlockSpec(a.shape, lambda t: (0,) * a.ndim)
    state = pl.BlockSpec((nb, SSM_CH), lambda t: (0, 0))
    y, h_re, h_im = pl.pallas_call(
        _s5_sample_kernel,
        out_shape=[jax.ShapeDtypeStruct((n_new * nb, D_MODEL), F32),
                   jax.ShapeDtypeStruct((nb, SSM_CH), F32), jax.ShapeDtypeStruct((nb, SSM_CH), F32)],
        grid=(n_new,),
        in_specs=[pl.BlockSpec((nb, D_MODEL), lambda t: (t, 0)), state, state,
                  full(sp["wb"]), full(sp["wc_re"]), full(sp["wc_im"]), full(sp["pw"]),
                  full(d), full(wglu), full(g), full(b)],
        out_specs=[pl.BlockSpec((nb, D_MODEL), lambda t: (t, 0)), state, state],
        compiler_params=_cparams(("arbitrary",)),
        name="s5_sample",
    )(x, h0_re.reshape(nb, SSM_CH), h0_im.reshape(nb, SSM_CH), sp["wb"], sp["wc_re"], sp["wc_im"], sp["pw"],
      d, wglu, g, b)
    return y, h_re.reshape(nb, SSM_GROUPS, SSM_STATE), h_im.reshape(nb, SSM_GROUPS, SSM_STATE)


def _ffn_prompt_kernel(x_ref, wup_ref, cw_ref, cb_ref, wdn_ref, g_ref, b_ref, y_ref, st_ref,
                       carry_ref, ua_ref, ub_ref, act_ref):
    tm = x_ref.shape[0]

    @pl.when(pl.program_id(1) == 0)
    def _():
        carry_ref[...] = jnp.zeros(carry_ref.shape, F32)

    x = x_ref[...]
    xb = x.astype(BF16)

    def conv(col0, scr):
        cols = slice(col0, col0 + FFN_CW)
        u = jnp.dot(xb, wup_ref[:, cols], preferred_element_type=F32)
        scr[0:SUBLANE, :] = carry_ref[:, cols]
        scr[SUBLANE:SUBLANE + tm, :] = u
        carry_ref[:, cols] = u[tm - SUBLANE:tm]
        p1 = scr[SUBLANE - 1:SUBLANE - 1 + tm, :]
        p2 = scr[SUBLANE - 2:SUBLANE - 2 + tm, :]
        return cb_ref[:, cols] + cw_ref[0:1, cols] * p2 + cw_ref[1:2, cols] * p1 + cw_ref[2:3, cols] * u

    for c in range(D_FF // FFN_CW):
        act = _gelu(conv(c * FFN_CW, ua_ref)) * conv(D_FF + c * FFN_CW, ub_ref)
        act_ref[:, c * FFN_CW:(c + 1) * FFN_CW] = act.astype(BF16)
    down = jnp.dot(act_ref[...], wdn_ref[...], preferred_element_type=F32)
    y_ref[...] = _layer_norm(ALPHA * x + down, g_ref[...], b_ref[...])
    st_ref[0] = carry_ref[...]


def _ffn_prompt(x, w, g, b, batch, seq):
    tm = ROW_TILE
    nt = seq // tm
    assert CONV_W - 1 <= SUBLANE and D_FF % FFN_CW == 0
    full = lambda a: pl.BlockSpec(a.shape, lambda bb, i: (0,) * a.ndim)
    y, st = pl.pallas_call(
        _ffn_prompt_kernel,
        out_shape=[jax.ShapeDtypeStruct((batch * seq, D_MODEL), F32),
                   jax.ShapeDtypeStruct((batch, SUBLANE, 2 * D_FF), F32)],
        grid=(batch, nt),
        in_specs=[pl.BlockSpec((tm, D_MODEL), lambda bb, i: (bb * nt + i, 0)),
                  full(w["w_up"]), full(w["conv_w"]), full(w["conv_b"]), full(w["w_down"]), full(g), full(b)],
        out_specs=[pl.BlockSpec((tm, D_MODEL), lambda bb, i: (bb * nt + i, 0)),
                   pl.BlockSpec((1, SUBLANE, 2 * D_FF), lambda bb, i: (bb, 0, 0))],
        scratch_shapes=[pltpu.VMEM((SUBLANE, 2 * D_FF), F32),
                        pltpu.VMEM((SUBLANE + tm, FFN_CW), F32), pltpu.VMEM((SUBLANE + tm, FFN_CW), F32),
                        pltpu.VMEM((tm, D_FF), BF16)],
        compiler_params=_cparams(("arbitrary", "arbitrary")),
        name="ffn_prompt",
    )(x, w["w_up"], w["conv_w"], w["conv_b"], w["w_down"], g, b)
    return y, st[:, SUBLANE - (CONV_W - 1):, :]


def _ffn_sample_kernel(x_ref, st_ref, wup_ref, cw_ref, cb_ref, wdn_ref, g_ref, b_ref, y_ref, ns_ref, act_ref):
    tm = x_ref.shape[0]
    w2 = 2 * D_FF

    @pl.when(pl.program_id(0) == 0)
    def _():
        ns_ref[...] = st_ref[...]

    x = x_ref[...]
    xb = x.astype(BF16)

    def conv(col0):
        cols = slice(col0, col0 + FFN_CW)
        cols1 = slice(w2 + col0, w2 + col0 + FFN_CW)
        u = jnp.dot(xb, wup_ref[:, cols], preferred_element_type=F32)
        p2 = ns_ref[:, cols]
        p1 = ns_ref[:, cols1]
        ns_ref[:, cols] = p1
        ns_ref[:, cols1] = u
        return cb_ref[:, cols] + cw_ref[0:1, cols] * p2 + cw_ref[1:2, cols] * p1 + cw_ref[2:3, cols] * u

    for c in range(D_FF // FFN_CW):
        act = _gelu(conv(c * FFN_CW)) * conv(D_FF + c * FFN_CW)
        act_ref[:, c * FFN_CW:(c + 1) * FFN_CW] = act.astype(BF16)
    down = jnp.dot(act_ref[...], wdn_ref[...], preferred_element_type=F32)
    y_ref[...] = _layer_norm(ALPHA * x + down, g_ref[...], b_ref[...])


def _ffn_sample(x, state, w, g, b, nb, n_new):
    assert CONV_W == 3
    full = lambda a: pl.BlockSpec(a.shape, lambda t: (0,) * a.ndim)
    st = state.reshape(nb, (CONV_W - 1) * 2 * D_FF)
    y, ns = pl.pallas_call(
        _ffn_sample_kernel,
        out_shape=[jax.ShapeDtypeStruct((n_new * nb, D_MODEL), F32), jax.ShapeDtypeStruct(st.shape, F32)],
        grid=(n_new,),
        in_specs=[pl.BlockSpec((nb, D_MODEL), lambda t: (t, 0)), full(st),
                  full(w["w_up"]), full(w["conv_w"]), full(w["conv_b"]), full(w["w_down"]), full(g), full(b)],
        out_specs=[pl.BlockSpec((nb, D_MODEL), lambda t: (t, 0)), full(st)],
        scratch_shapes=[pltpu.VMEM((nb, D_FF), BF16)],
        compiler_params=_cparams(("arbitrary",)),
        name="ffn_sample",
    )(x, st, w["w_up"], w["conv_w"], w["conv_b"], w["w_down"], g, b)
    return y, ns.reshape(nb, CONV_W - 1, 2 * D_FF)


def _attn_weights(w_in, q_norm, w_q_b, kv_norm, w_kv_b, w_out):
    o1 = Q_RANK
    o2 = o1 + KV_RANK
    o3 = o2 + ROPE_DIM
    kpe = w_in[:, o2:o3]
    w_in_r = jnp.concatenate([w_in[:, :o2], w_in[:, o3:]] + [kpe] * (LANE // ROPE_DIM), axis=1).astype(BF16)
    assert w_in_r.shape[1] == W_IN_COLS
    qb = w_q_b.reshape(Q_RANK, MLA_HEADS, NOPE_DIM + ROPE_DIM)
    nope = jnp.pad(qb[:, :, :NOPE_DIM], ((0, 0), (0, 0), (0, LANE - NOPE_DIM))).reshape(Q_RANK, MLA_HEADS * LANE)
    ropew = qb[:, :, NOPE_DIM:].reshape(Q_RANK, MLA_HEADS * ROPE_DIM)
    w_q_b_r = jnp.concatenate([nope, ropew], axis=1).astype(BF16)
    wk = jnp.transpose(w_kv_b[:, :, :NOPE_DIM], (1, 2, 0))
    wk = jnp.pad(wk, ((0, 0), (0, LANE - NOPE_DIM), (0, 0))).astype(BF16)
    wv = jnp.transpose(w_kv_b[:, :, NOPE_DIM:], (1, 0, 2))
    wv_lo = jnp.pad(wv, ((0, 0), (0, 0), (0, LANE - V_DIM)))
    wv_hi = jnp.pad(wv, ((0, 0), (0, 0), (LANE - V_DIM, 0)))
    odd = (jnp.arange(MLA_HEADS) % 2 == 1)[:, None, None]
    wvp = jnp.where(odd, wv_hi, wv_lo).astype(BF16)
    return dict(w_in=w_in_r, q_norm=q_norm.reshape(1, Q_RANK), w_q_b=w_q_b_r, wk=wk,
                kv_norm=kv_norm.reshape(1, KV_RANK), wv=wvp, w_out=w_out.astype(BF16))


def _ffn_weights(w_up, conv_w, conv_b, w_down):
    return dict(w_up=w_up.astype(BF16), conv_w=conv_w, conv_b=conv_b.reshape(1, 2 * D_FF), w_down=w_down.astype(BF16))


def _to_time_major(a):
    return jnp.swapaxes(a, 0, 1).reshape((a.shape[0] * a.shape[1],) + a.shape[2:])


def _from_time_major(a, nb):
    return jnp.swapaxes(a.reshape((a.shape[0] // nb, nb) + a.shape[1:]), 0, 1)


def kernel(x_prompt, x_sample, cache_mla_latent, cache_mla_rope, cache_moba_k, cache_moba_v, state_s5_re, state_s5_im, state_ffn_conv, page_table, ln_mix_g, ln_mix_b, ln_ffn_g, ln_ffn_b, att_w_in, mla_q_norm, mla_w_q_b, mla_kv_norm, mla_w_kv_b, att_w_out, s5_a_re, s5_a_im, s5_log_dt, s5_b_re, s5_b_im, s5_c_re, s5_c_im, s5_d, s5_w_glu, ffn_w_up, ffn_conv_w, ffn_conv_b, ffn_w_down):
    batch, seq, _ = x_prompt.shape
    nb, n_new, _ = x_sample.shape
    past_len = page_table.shape[1] * PAGE_SIZE
    depth = ln_mix_g.shape[0]
    assert seq % MLA_TK == 0 and nb % SUBLANE == 0 and (nb * n_new) % ROW_TILE == 0

    xp = x_prompt.reshape(batch * seq, D_MODEL)
    xs = _to_time_major(x_sample)
    tabs_p = _rope_tables(seq, 0)
    tabs_s = [jnp.repeat(t[:n_new], nb, axis=0) for t in _rope_tables(SUBLANE * pl.cdiv(n_new, SUBLANE), past_len)]
    row = lambda v: v.reshape(1, -1)
    pool_rope_t = jnp.swapaxes(cache_mla_rope, 2, 3)

    def tokens_minor(pool):
        return jnp.transpose(pool, (0, 1, 3, 4, 2)).reshape(pool.shape[0], pool.shape[1], MOBA_WIDTH, PAGE_SIZE)

    pool_kt, pool_vt = tokens_minor(cache_moba_k), tokens_minor(cache_moba_v)

    outs_p ={k: [] for k in ("lat", "rope", "mk", "mv", "sre", "sim", "conv")}
    outs_s = {k: [] for k in ("lat", "rope", "mk", "mv", "sre", "sim", "conv")}
    for l in range(depth):
        g_mix, b_mix = row(ln_mix_g[l]), row(ln_mix_b[l])
        if l % 2 == 0:
            a = l // 2
            w = _attn_weights(att_w_in[a], mla_q_norm[a], mla_w_q_b[a], mla_kv_norm[a], mla_w_kv_b[a], att_w_out[a])
            qcat, kcat, lat, rp, qm, km, kaug, vm, vmb, kmean = _attn_proj(xp, w, tabs_p, seq)
            o_mla = _mla_prompt(qcat, kcat, batch, seq)
            qaug = _moba_select(qm, kmean, batch, seq)
            o_moba = _moba_prompt(qaug, kaug, vmb, batch, seq)
            xp = _attn_out(xp, o_mla, o_moba, w, g_mix, b_mix)
            outs_p["lat"].append(lat.reshape(batch, seq, KV_RANK))
            outs_p["rope"].append(rp.reshape(batch, seq, ROPE_DIM))
            outs_p["mk"].append(km.reshape(batch, seq, MOBA_HEADS, MOBA_DIM))
            outs_p["mv"].append(vm.reshape(batch, seq, MOBA_HEADS, MOBA_DIM))
            qcat, _, lat, rp, qm, km, _, vm, _, _ = _attn_proj(xs, w, tabs_s, nb * n_new)
            lat_b, rp_b = _from_time_major(lat, nb), _from_time_major(rp, nb)
            km_b, vm_b = _from_time_major(km, nb), _from_time_major(vm, nb)
            qm_b = _from_time_major(qm, nb)
            qc = qcat.reshape(n_new, nb // LANE, MLA_HEADS, LANE, 2 * LANE)
            qc = jnp.transpose(qc, (1, 3, 0, 2, 4)).reshape(nb, n_new * MLA_HEADS, 2 * LANE)
            qlat = qc[:, :, :KV_RANK]
            qr4 = qc[:, :, KV_RANK:].reshape(nb, n_new, MLA_HEADS, LANE // ROPE_DIM, ROPE_DIM)
            grp = (jnp.arange(MLA_HEADS) % (LANE // ROPE_DIM))
            qrope = jnp.take_along_axis(qr4, grp[None, None, :, None, None], axis=3)[:, :, :, 0, :]
            qrope = qrope.reshape(nb, n_new * MLA_HEADS, ROPE_DIM)
            o_mla = _mla_sample(page_table, qlat, qrope, lat_b, rp_b.astype(BF16), cache_mla_latent, pool_rope_t, a)
            q4 = qm_b.reshape(nb, n_new, MOBA_HEADS, MOBA_DIM)
            eye = jnp.eye(MOBA_HEADS, dtype=F32)
            qbdf = jnp.einsum("bthd,hk->bthkd", q4, eye).reshape(nb, n_new * MOBA_HEADS, MOBA_WIDTH)
            o_moba = _moba_sample(page_table, qbdf.astype(BF16), qbdf, km_b.astype(BF16), vm_b.astype(BF16),
                                  pool_kt, pool_vt, a)
            om = o_mla.reshape(nb // LANE, LANE, n_new, MLA_HEADS, KV_RANK)
            om = jnp.transpose(om, (2, 0, 3, 1, 4)).reshape(n_new * nb // LANE, MLA_HEADS, LANE, KV_RANK)
            xs = _attn_out(xs, om, _to_time_major(o_moba), w, g_mix, b_mix)
            outs_s["lat"].append(lat_b)
            outs_s["rope"].append(rp_b)
            outs_s["mk"].append(km_b.reshape(nb, n_new, MOBA_HEADS, MOBA_DIM))
            outs_s["mv"].append(vm_b.reshape(nb, n_new, MOBA_HEADS, MOBA_DIM))
        else:
            s = l // 2
            sp = _s5_params(s5_a_re[s], s5_a_im[s], s5_log_dt[s], s5_b_re[s], s5_b_im[s], s5_c_re[s], s5_c_im[s])
            d, wglu = row(s5_d[s]), s5_w_glu[s].astype(BF16)
            xp, hre, him = _s5_prompt(xp, sp, d, wglu, g_mix, b_mix, batch, seq)
            outs_p["sre"].append(hre)
            outs_p["sim"].append(him)
            xs, hre, him = _s5_sample(xs, state_s5_re[s], state_s5_im[s], sp, d, wglu, g_mix, b_mix, nb, n_new)
            outs_s["sre"].append(hre)
            outs_s["sim"].append(him)
        fw = _ffn_weights(ffn_w_up[l], ffn_conv_w[l], ffn_conv_b[l], ffn_w_down[l])
        g_ffn, b_ffn = row(ln_ffn_g[l]), row(ln_ffn_b[l])
        xp, cp = _ffn_prompt(xp, fw, g_ffn, b_ffn, batch, seq)
        xs, cs = _ffn_sample(xs, state_ffn_conv[l], fw, g_ffn, b_ffn, nb, n_new)
        outs_p["conv"].append(cp)
        outs_s["conv"].append(cs)

    keys = ("lat", "rope", "mk", "mv", "sre", "sim", "conv")
    return ((xp.reshape(batch, seq, D_MODEL), _from_time_major(xs, nb))
            + tuple(jnp.stack(outs_p[k]) for k in keys) + tuple(jnp.stack(outs_s[k]) for k in keys))
```
